```python
import jax, jax.numpy as jnp
from jax import lax
import numpy as np

D_MODEL = 4096
BATCH = 2
SEQ = 4096
DEPTH = 2

HEAD_DIM = D_MODEL // 32
ROPE_THETA = 10000.0
EPS = 1e-6
NEG = -1e30
NSA_HEADS = 12
NSA_KV_HEADS = 2
NSA_GROUP = NSA_HEADS // NSA_KV_HEADS
NSA_BLOCK = 64
NSA_TOPN = 16
NSA_WINDOW = 512
NSA_Q_BLOCK = 64
NSA_CMP_HIDDEN = 256
NSA_FORCED_SCORE = NSA_GROUP + 1.0
DIL_PAIRS = ((128, 1), (512, 4), (2048, 16))
DIL_HEADS_PER_PAIR = 4
DIL_HEADS = DIL_HEADS_PER_PAIR * len(DIL_PAIRS)
DIL_Q_BLOCK = 128
SB_HEADS = 8
SB_Q_BLOCK = 128
D_FF = 4 * D_MODEL

NSA_Q_W = NSA_HEADS * HEAD_DIM
NSA_KV_W = NSA_KV_HEADS * HEAD_DIM
NSA_GATE_W = NSA_HEADS * 3
DIL_W = DIL_HEADS * HEAD_DIM
SB_W = SB_HEADS * HEAD_DIM
PROJ_SIZES = (NSA_Q_W,) + (NSA_KV_W,) * 6 + (NSA_GATE_W,) + (DIL_W,) * 3 + (SB_W,) * 3
D_PROJ = sum(PROJ_SIZES)
SPLIT_POINTS = tuple(sum(PROJ_SIZES[:i + 1]) for i in range(len(PROJ_SIZES) - 1))

kernel_name = "hybrid_nsa_dilated_stickbreaking_block"


def rms_norm(x, g):
    x32 = x.astype(jnp.float32)
    y = x32 * lax.rsqrt(jnp.mean(x32 * x32, axis=-1, keepdims=True) + EPS)
    return (y * g.astype(jnp.float32)).astype(x.dtype)


def rope_tables(seq_len):
    inv = 1.0 / (ROPE_THETA ** (jnp.arange(0, HEAD_DIM, 2, dtype=jnp.float32) / HEAD_DIM))
    ang = jnp.arange(seq_len, dtype=jnp.float32)[:, None] * inv[None, :]
    return jnp.cos(ang), jnp.sin(ang)


def apply_rope(x, cos, sin):
    x32 = x.astype(jnp.float32)
    half = HEAD_DIM // 2
    x1, x2 = x32[..., :half], x32[..., half:]
    c, s = cos[None, :, None, :], sin[None, :, None, :]
    return jnp.concatenate([x1 * c - x2 * s, x1 * s + x2 * c], axis=-1).astype(x.dtype)


def nsa_compress(kv, pe, w1, w2):
    B, T, G, D = kv.shape
    nb = T // NSA_BLOCK
    blk = kv.reshape(B, nb, NSA_BLOCK, G, D) + pe[None, None, :, None, :].astype(kv.dtype)
    blk = blk.transpose(0, 1, 3, 2, 4).reshape(B, nb, G, NSA_BLOCK * D)
    return jax.nn.silu(blk @ w1) @ w2


def nsa_mixer(q, k_cmp, v_cmp, k_slc, v_slc, k_win, v_win, gate, pe_k, pe_v, w_ck1, w_ck2, w_cv1, w_cv2):
    B, T, H, D = q.shape
    G, R = NSA_KV_HEADS, NSA_GROUP
    scale = D ** -0.5
    nb = T // NSA_BLOCK
    pos = jnp.arange(T)
    qg = q.reshape(B, T, G, R, D)

    kc = nsa_compress(k_cmp, pe_k, w_ck1, w_ck2)
    vc = nsa_compress(v_cmp, pe_v, w_cv1, w_cv2)
    s_c = jnp.einsum('btgrd,bngd->bgrtn', qg, kc).astype(jnp.float32) * scale
    blk = jnp.arange(nb)
    cmp_ok = ((blk + 1) * NSA_BLOCK - 1)[None, :] <= pos[:, None]
    p_c = jax.nn.softmax(jnp.where(cmp_ok, s_c, NEG), axis=-1)
    p_c = p_c * jnp.any(cmp_ok, axis=-1)[:, None].astype(jnp.float32)
    o_cmp = jnp.einsum('bgrtn,bngd->btgrd', p_c.astype(vc.dtype), vc)

    imp = p_c.sum(axis=2)
    cur = pos // NSA_BLOCK
    started = blk[None, :] <= cur[:, None]
    forced = (blk[None, :] == 0) | (blk[None, :] == cur[:, None]) | (blk[None, :] == cur[:, None] - 1)
    imp = jnp.where(forced, NSA_FORCED_SCORE, jnp.where(started, imp, -1.0))
    n_sel = min(NSA_TOPN, nb)
    _, sel = lax.top_k(imp, n_sel)

    QB = NSA_Q_BLOCK
    nqb = T // QB
    ks_len = n_sel * NSA_BLOCK
    kT_s = k_slc.transpose(0, 2, 1, 3)
    vT_s = v_slc.transpose(0, 2, 1, 3)
    kw_pad = jnp.pad(k_win, ((0, 0), (NSA_WINDOW, 0), (0, 0), (0, 0)))
    vw_pad = jnp.pad(v_win, ((0, 0), (NSA_WINDOW, 0), (0, 0), (0, 0)))
    q_blocks = qg.reshape(B, nqb, QB, G, R, D).transpose(1, 0, 2, 3, 4, 5)
    sel_blocks = sel.reshape(B, G, nqb, QB, n_sel).transpose(2, 0, 1, 3, 4)
    gather = jax.vmap(jax.vmap(lambda a, idx: a[idx]))

    def body(args):
        i, qb, sb = args
        qpos = i * QB + jnp.arange(QB)
        tok = (sb[..., None] * NSA_BLOCK + jnp.arange(NSA_BLOCK)).reshape(B, G, QB * ks_len)
        ks = gather(kT_s, tok).reshape(B, G, QB, ks_len, D)
        vs = gather(vT_s, tok).reshape(B, G, QB, ks_len, D)
        s = jnp.einsum('bqgrd,bgqkd->bgrqk', qb, ks).astype(jnp.float32) * scale
        ok_s = tok.reshape(B, G, QB, ks_len) <= qpos[:, None]
        p = jax.nn.softmax(jnp.where(ok_s[:, :, None], s, NEG), axis=-1)
        o_s = jnp.einsum('bgrqk,bgqkd->bqgrd', p.astype(vs.dtype), vs)
        kw = lax.dynamic_slice_in_dim(kw_pad, i * QB, QB + NSA_WINDOW, axis=1)
        vw = lax.dynamic_slice_in_dim(vw_pad, i * QB, QB + NSA_WINDOW, axis=1)
        kpos = i * QB - NSA_WINDOW + jnp.arange(QB + NSA_WINDOW)
        ok_w = (kpos[None, :] >= 0) & (kpos[None, :] <= qpos[:, None]) & (kpos[None, :] > qpos[:, None] - NSA_WINDOW)
        s = jnp.einsum('bqgrd,bkgd->bgrqk', qb, kw).astype(jnp.float32) * scale
        p = jax.nn.softmax(jnp.where(ok_w, s, NEG), axis=-1)
        o_w = jnp.einsum('bgrqk,bkgd->bqgrd', p.astype(vw.dtype), vw)
        return o_s, o_w

    o_slc, o_win = lax.map(body, (jnp.arange(nqb), q_blocks, sel_blocks))
    o_slc = o_slc.transpose(1, 0, 2, 3, 4, 5).reshape(B, T, G, R, D)
    o_win = o_win.transpose(1, 0, 2, 3, 4, 5).reshape(B, T, G, R, D)

    g = jax.nn.sigmoid(gate.astype(jnp.float32)).reshape(B, T, G, R, 3, 1)
    o = (g[..., 0, :] * o_cmp.astype(jnp.float32) + g[..., 1, :] * o_slc.astype(jnp.float32)
         + g[..., 2, :] * o_win.astype(jnp.float32))
    return o.reshape(B, T, H * D).astype(q.dtype)


def dilated_group(q, k, v, window, dilation):
    B, T, h, D = q.shape
    L = window // dilation
    M = T // dilation
    QB = DIL_Q_BLOCK
    nb = -(-M // QB)
    Mp = nb * QB

    def to_sub(a, front):
        a = a.reshape(B, M, dilation, h, D).transpose(0, 2, 1, 3, 4)
        return jnp.pad(a, ((0, 0), (0, 0), (front, Mp - M), (0, 0), (0, 0)))

    qs = to_sub(q, 0).reshape(B, dilation, nb, QB, h, D)
    kidx = jnp.arange(nb)[:, None] * QB + jnp.arange(QB + L)[None, :]
    ks = to_sub(k, L)[:, :, kidx]
    vs = to_sub(v, L)[:, :, kidx]
    s = jnp.einsum('brnqhd,brnkhd->brnhqk', qs, ks).astype(jnp.float32) * (D ** -0.5)
    jj = jnp.arange(QB + L)[None, None, :]
    qq = jnp.arange(QB)[None, :, None]
    ii = jnp.arange(nb)[:, None, None]
    ok = (jj >= qq) & (jj <= qq + L) & (ii * QB + jj - L >= 0)
    s = jnp.where(ok[:, None], s, NEG)
    lse = jax.nn.logsumexp(s, axis=-1, keepdims=True)
    p = jnp.exp(s - lse)
    o = jnp.einsum('brnhqk,brnkhd->brnqhd', p.astype(vs.dtype), vs)
    o = o.reshape(B, dilation, Mp, h, D)[:, :, :M].transpose(0, 2, 1, 3, 4).reshape(B, T, h, D)
    lse = lse[..., 0].transpose(0, 1, 2, 4, 3).reshape(B, dilation, Mp, h)[:, :, :M]
    lse = lse.transpose(0, 2, 1, 3).reshape(B, T, h)
    return o, lse


def dilated_mixer(q, k, v):
    B, T, H, D = q.shape
    outs, lses = [], []
    for g, (w, d) in enumerate(DIL_PAIRS):
        sl = slice(g * DIL_HEADS_PER_PAIR, (g + 1) * DIL_HEADS_PER_PAIR)
        o, l = dilated_group(q[:, :, sl], k[:, :, sl], v[:, :, sl], w, d)
        outs.append(o)
        lses.append(l)
    alpha = jax.nn.softmax(jnp.stack(lses, axis=0), axis=0)
    o = jnp.concatenate([outs[g].astype(jnp.float32) * alpha[g][..., None] for g in range(len(DIL_PAIRS))], axis=2)
    return o.reshape(B, T, H * D).astype(q.dtype)


def stick_breaking_mixer(q, k, v):
    B, T, h, D = q.shape
    QB = SB_Q_BLOCK
    nqb = T // QB
    scale = D ** -0.5
    kpos = jnp.arange(T)
    q_blocks = q.reshape(B, nqb, QB, h, D).transpose(1, 0, 2, 3, 4)

    def body(args):
        i, qb = args
        qpos = i * QB + jnp.arange(QB)
        z = jnp.einsum('bqhd,bkhd->bhqk', qb, k).astype(jnp.float32) * scale
        before = kpos[None, :] < qpos[:, None]
        log_beta = jax.nn.log_sigmoid(z)
        log_1m = jnp.where(before, jax.nn.log_sigmoid(-z), 0.0)
        log_keep = lax.cumsum(log_1m, axis=3, reverse=True) - log_1m
        a = jnp.where(before, jnp.exp(log_beta + log_keep), 0.0)
        return jnp.einsum('bhqk,bkhd->bqhd', a.astype(v.dtype), v)

    o = lax.map(body, (jnp.arange(nqb), q_blocks))
    return o.transpose(1, 0, 2, 3, 4).reshape(B, T, h * D)


def hybrid_layer(x, cos, sin, w_in, g_mix, w_out, g_mlp, w_up, w_down, pe_k, pe_v, w_ck1, w_ck2, w_cv1, w_cv2):
    B, T, _ = x.shape
    h = rms_norm(x, g_mix)
    parts = jnp.split(h @ w_in, SPLIT_POINTS, axis=-1)
    (nq, nkc, nvc, nks, nvs, nkw, nvw, ngate, dq, dk, dv, sq, sk, sv) = parts
    heads = lambda a, n: a.reshape(B, T, n, HEAD_DIM)
    rope = lambda a: apply_rope(a, cos, sin)
    o_a = nsa_mixer(rope(heads(nq, NSA_HEADS)),
                    rope(heads(nkc, NSA_KV_HEADS)), heads(nvc, NSA_KV_HEADS),
                    rope(heads(nks, NSA_KV_HEADS)), heads(nvs, NSA_KV_HEADS),
                    rope(heads(nkw, NSA_KV_HEADS)), heads(nvw, NSA_KV_HEADS),
                    ngate, pe_k, pe_v, w_ck1, w_ck2, w_cv1, w_cv2)
    o_b = dilated_mixer(rope(heads(dq, DIL_HEADS)), rope(heads(dk, DIL_HEADS)), heads(dv, DIL_HEADS))
    o_c = stick_breaking_mixer(heads(sq, SB_HEADS), heads(sk, SB_HEADS), heads(sv, SB_HEADS))
    x = x + jnp.concatenate([o_a, o_b, o_c.astype(x.dtype)], axis=-1) @ w_out
    h2 = rms_norm(x, g_mlp)
    return x + jnp.square(jax.nn.relu(h2 @ w_up)) @ w_down


def setup_inputs(seed: int = 0) -> dict:
    key = jax.random.key(seed)
    ks = jax.random.split(key, 15)
    f32 = jnp.float32
    nrm = lambda k, shape, s: jax.random.normal(k, shape, f32) * s
    return {
        "x": nrm(ks[0], (BATCH, SEQ, D_MODEL), 1.0),
        "w_in": nrm(ks[1], (DEPTH, D_MODEL, D_PROJ), D_MODEL ** -0.5),
        "w_out": nrm(ks[2], (DEPTH, D_MODEL, D_MODEL), D_MODEL ** -0.5),
        "norm_mix": 1.0 + nrm(ks[3], (DEPTH, D_MODEL), 0.02),
        "norm_mlp": 1.0 + nrm(ks[4], (DEPTH, D_MODEL), 0.02),
        "w_up": nrm(ks[5], (DEPTH, D_MODEL, D_FF), D_MODEL ** -0.5),
        "w_down": nrm(ks[6], (DEPTH, D_FF, D_MODEL), D_FF ** -0.5),
        "nsa_pe_k": nrm(ks[7], (DEPTH, NSA_BLOCK, HEAD_DIM), 0.1),
        "nsa_pe_v": nrm(ks[8], (DEPTH, NSA_BLOCK, HEAD_DIM), 0.1),
        "nsa_w_ck1": nrm(ks[9], (DEPTH, NSA_BLOCK * HEAD_DIM, NSA_CMP_HIDDEN), (NSA_BLOCK * HEAD_DIM) ** -0.5),
        "nsa_w_ck2": nrm(ks[10], (DEPTH, NSA_CMP_HIDDEN, HEAD_DIM), NSA_CMP_HIDDEN ** -0.5),
        "nsa_w_cv1": nrm(ks[11], (DEPTH, NSA_BLOCK * HEAD_DIM, NSA_CMP_HIDDEN), (NSA_BLOCK * HEAD_DIM) ** -0.5),
        "nsa_w_cv2": nrm(ks[12], (DEPTH, NSA_CMP_HIDDEN, HEAD_DIM), NSA_CMP_HIDDEN ** -0.5),
        "final_norm": 1.0 + nrm(ks[13], (D_MODEL,), 0.02),
    }


def reference(x, w_in, w_out, norm_mix, norm_mlp, w_up, w_down, nsa_pe_k, nsa_pe_v,
              nsa_w_ck1, nsa_w_ck2, nsa_w_cv1, nsa_w_cv2, final_norm):
    cos, sin = rope_tables(x.shape[1])
    for layer in range(DEPTH):
        x = hybrid_layer(x, cos, sin, w_in[layer], norm_mix[layer], w_out[layer], norm_mlp[layer],
                         w_up[layer], w_down[layer], nsa_pe_k[layer], nsa_pe_v[layer],
                         nsa_w_ck1[layer], nsa_w_ck2[layer], nsa_w_cv1[layer], nsa_w_cv2[layer])
    return rms_norm(x, final_norm)
```

```python
import functools

import jax
import jax.numpy as jnp
from jax import lax
from jax.experimental import pallas as pl
from jax.experimental.pallas import tpu as pltpu

F32 = jnp.float32
BF16 = jnp.bfloat16

D_MODEL = 4096
HEAD_DIM = 128
ROPE_THETA = 10000.0
EPS = 1e-6
NEG = -1e30
NSA_HEADS = 12
NSA_KV_HEADS = 2
NSA_GROUP = NSA_HEADS // NSA_KV_HEADS
NSA_BLOCK = 64
NSA_TOPN = 16
NSA_WINDOW = 512
NSA_FORCED_SCORE = NSA_GROUP + 1.0
DIL_PAIRS = ((128, 1), (512, 4), (2048, 16))
DIL_HEADS_PER_PAIR = 4
DIL_HEADS = DIL_HEADS_PER_PAIR * len(DIL_PAIRS)
SB_HEADS = 8
D_FF = 4 * D_MODEL
SCALE = HEAD_DIM ** -0.5

LANES = 128
QB = 128
KB = 128
VMEM_LIMIT = 56 * 1024 * 1024

A_COLS = 80 * LANES
A_TN = 512
CB_NSA_Q, CB_KSLC, CB_KWIN, CB_VSLC = 0, 12, 14, 16
CB_DQ, CB_DK, CB_DV = 20, 32, 44
CB_SQ, CB_SK, CB_SV = 56, 64, 72
B_CHUNKS = 6


def _params(sem):
    return pltpu.CompilerParams(dimension_semantics=sem, vmem_limit_bytes=VMEM_LIMIT)


def _rmsnorm_kernel(x_ref, g_ref, o_ref):
    x = x_ref[...]
    ms = jnp.mean(x * x, axis=-1, keepdims=True)
    o_ref[...] = (x * lax.rsqrt(ms + EPS) * g_ref[...]).astype(o_ref.dtype)


def _rmsnorm(x2d, g, out_dtype):
    m, d = x2d.shape
    tm = 256
    return pl.pallas_call(
        _rmsnorm_kernel,
        grid=(m // tm,),
        in_specs=[pl.BlockSpec((tm, d), lambda i: (i, 0)),
                  pl.BlockSpec((1, d), lambda i: (0, 0))],
        out_specs=pl.BlockSpec((tm, d), lambda i: (i, 0)),
        out_shape=jax.ShapeDtypeStruct((m, d), out_dtype),
        compiler_params=_params(("parallel",)),
        name="rmsnorm",
    )(x2d, g.reshape(1, d))


def _rope(xc, cos, sin):
    return xc * cos + pltpu.roll(xc, HEAD_DIM // 2, 1) * sin


def _proj_a_kernel(a_ref, w_ref, cos_ref, sin_ref, o_ref):
    j = pl.program_id(1)
    acc = jnp.dot(a_ref[...], w_ref[...], preferred_element_type=F32)
    roped = (j < 4) | ((j >= 5) & (j < 11))

    @pl.when(roped)
    def _():
        cos, sin = cos_ref[...], sin_ref[...]
        for c in range(A_TN // LANES):
            sl = slice(c * LANES, (c + 1) * LANES)
            o_ref[:, sl] = _rope(acc[:, sl], cos, sin).astype(o_ref.dtype)

    @pl.when(jnp.logical_not(roped))
    def _():
        o_ref[...] = acc.astype(o_ref.dtype)


def _proj_b_kernel(a_ref, w_ref, cos_ref, sin_ref, o_ref):
    acc = jnp.dot(a_ref[...], w_ref[...], preferred_element_type=F32)
    cos, sin = cos_ref[...], sin_ref[...]
    for c in range(B_CHUNKS):
        xc = acc[:, c * LANES:(c + 1) * LANES]
        o_ref[c] = _rope(xc, cos, sin) if c < NSA_KV_HEADS else xc


def _proj_a(h, w, cosf, sinf, seq):
    m, k = h.shape
    tm = 1024
    t_blocks = seq // tm
    return pl.pallas_call(
        _proj_a_kernel,
        grid=(m // tm, A_COLS // A_TN),
        in_specs=[pl.BlockSpec((tm, k), lambda i, j: (i, 0)),
                  pl.BlockSpec((k, A_TN), lambda i, j: (0, j)),
                  pl.BlockSpec((tm, LANES), lambda i, j: (i % t_blocks, 0)),
                  pl.BlockSpec((tm, LANES), lambda i, j: (i % t_blocks, 0))],
        out_specs=pl.BlockSpec((tm, A_TN), lambda i, j: (i, j)),
        out_shape=jax.ShapeDtypeStruct((m, A_COLS), BF16),
        compiler_params=_params(("parallel", "arbitrary")),
        name="proj_a",
    )(h, w, cosf, sinf)


def _proj_b(h, w, cosf, sinf, seq):
    m, k = h.shape
    tm = 1024
    t_blocks = seq // tm
    n = B_CHUNKS * LANES
    return pl.pallas_call(
        _proj_b_kernel,
        grid=(m // tm,),
        in_specs=[pl.BlockSpec((tm, k), lambda i: (i, 0)),
                  pl.BlockSpec((k, n), lambda i: (0, 0)),
                  pl.BlockSpec((tm, LANES), lambda i: (i % t_blocks, 0)),
                  pl.BlockSpec((tm, LANES), lambda i: (i % t_blocks, 0))],
        out_specs=pl.BlockSpec((B_CHUNKS, tm, LANES), lambda i: (0, i, 0)),
        out_shape=jax.ShapeDtypeStruct((B_CHUNKS, m, LANES), F32),
        compiler_params=_params(("parallel",)),
        name="proj_b",
    )(h, w, cosf, sinf)


def _compress_kernel(x_ref, pe_ref, w1_ref, w2_ref, o_ref):
    blk = (x_ref[...] + pe_ref[0]).astype(BF16)
    hid = jnp.dot(blk, w1_ref[0], preferred_element_type=F32)
    hid = hid * jax.nn.sigmoid(hid)
    o_ref[0] = jnp.dot(hid.astype(BF16), w2_ref[0], preferred_element_type=F32)


def _compress(x_kv, pe, w1, w2):
    rows = x_kv.shape[0] // 2
    kdim = x_kv.shape[1]
    hidden = w1.shape[-1]
    return pl.pallas_call(
        _compress_kernel,
        grid=(2,),
        in_specs=[pl.BlockSpec((rows, kdim), lambda i: (i, 0)),
                  pl.BlockSpec((1, 1, kdim), lambda i: (i, 0, 0)),
                  pl.BlockSpec((1, kdim, hidden), lambda i: (i, 0, 0)),
                  pl.BlockSpec((1, hidden, HEAD_DIM), lambda i: (i, 0, 0))],
        out_specs=pl.BlockSpec((1, rows, HEAD_DIM), lambda i: (i, 0, 0)),
        out_shape=jax.ShapeDtypeStruct((2, rows, HEAD_DIM), F32),
        compiler_params=_params(("parallel",)),
        name="nsa_compress",
    )(x_kv, pe, w1, w2)


NB_PAD = 128


def _nt_dot(a, b):
    return lax.dot_general(a, b, (((1,), (1,)), ((), ())), preferred_element_type=F32)


def _nsa_kernel(q_ref, ks_ref, kw_ref, vst_ref, vwt_ref, kc_ref, vct_ref, gate_ref, o_ref,
                val_ref, sel_ref, gt_ref, m_ref, l_ref, acc_ref):
    i = pl.program_id(2)
    n_blocks = NB_PAD // 2
    lane_t = lax.broadcasted_iota(jnp.int32, (1, QB), 1)
    t_row = i * QB + lane_t
    sub_n = lax.broadcasted_iota(jnp.int32, (NB_PAD, QB), 0)
    t_full = i * QB + lax.broadcasted_iota(jnp.int32, (NB_PAD, QB), 1)
    t_loc = lax.broadcasted_iota(jnp.int32, (KB, QB), 1)
    s_loc = lax.broadcasted_iota(jnp.int32, (KB, QB), 0)

    cmp_ok = (sub_n * NSA_BLOCK + (NSA_BLOCK - 1) <= t_full) & (sub_n < n_blocks)
    any_ok = jnp.where(t_row >= NSA_BLOCK - 1, 1.0, 0.0)
    kc = kc_ref[0]
    vct = vct_ref[0]
    imp = jnp.zeros((NB_PAD, QB), F32)
    for r in range(NSA_GROUP):
        q_r = q_ref[:, r * LANES:(r + 1) * LANES]
        s = jnp.where(cmp_ok, _nt_dot(kc, q_r) * SCALE, NEG)
        e = jnp.exp(s - jnp.max(s, axis=0, keepdims=True))
        p = e / jnp.sum(e, axis=0, keepdims=True) * any_ok
        imp = imp + p
        acc_ref[0, r] = jnp.dot(vct, p.astype(BF16), preferred_element_type=F32)

    cur = jnp.right_shift(t_full, 6)
    started = sub_n <= cur
    forced = (sub_n == 0) | (sub_n == cur) | (sub_n == cur - 1)
    val = jnp.where(forced, NSA_FORCED_SCORE, jnp.where(started, imp, -1.0))
    val_ref[...] = val
    rank = jnp.zeros((NB_PAD, QB), F32)
    for n2 in range(n_blocks):
        row = val_ref[n2:n2 + 1, :]
        rank = rank + jnp.where(sub_n > n2, jnp.where(row >= val, 1.0, 0.0),
                                jnp.where(row > val, 1.0, 0.0))
    sel_ref[...] = jnp.where(rank < NSA_TOPN, 1.0, 0.0)

    def tile(branch, k_chunk, vt_chunk, mask, first):
        for r in range(NSA_GROUP):
            q_r = q_ref[:, r * LANES:(r + 1) * LANES]
            s = jnp.where(mask, _nt_dot(k_chunk, q_r) * SCALE, NEG)
            m_cur = jnp.max(s, axis=0, keepdims=True)
            idx = branch * NSA_GROUP + r
            if first:
                m_new = m_cur
                p = jnp.exp(s - m_new)
                l_ref[idx] = jnp.sum(p, axis=0, keepdims=True)
                acc_ref[branch, r] = jnp.dot(vt_chunk, p.astype(BF16), preferred_element_type=F32)
            else:
                m_old = m_ref[idx]
                m_new = jnp.maximum(m_old, m_cur)
                alpha = jnp.exp(m_old - m_new)
                p = jnp.exp(s - m_new)
                l_ref[idx] = alpha * l_ref[idx] + jnp.sum(p, axis=0, keepdims=True)
                acc_ref[branch, r] = alpha * acc_ref[branch, r] + jnp.dot(
                    vt_chunk, p.astype(BF16), preferred_element_type=F32)
            m_ref[idx] = m_new

    def sel_mask(c):
        lo = sel_ref[pl.ds(2 * c, 1), :]
        hi = sel_ref[pl.ds(2 * c + 1, 1), :]
        return jnp.where(s_loc < NSA_BLOCK, lo, hi) > 0.5

    causal = s_loc <= t_loc
    q0 = pl.multiple_of(i * QB, QB)

    tile(1, ks_ref[pl.ds(q0, KB), :], vst_ref[i], sel_mask(i) & causal, True)

    def slc_body(c, carry):
        k0 = pl.multiple_of(c * KB, KB)
        tile(1, ks_ref[pl.ds(k0, KB), :], vst_ref[c], sel_mask(c), False)
        return carry

    lax.fori_loop(0, i, slc_body, 0)

    tile(2, kw_ref[pl.ds(q0, KB), :], vwt_ref[i], causal, True)
    n_win = NSA_WINDOW // KB

    def win_body(c, carry):
        kc_idx = i - n_win + c
        k0 = pl.multiple_of(kc_idx * KB, KB)
        mask = (t_loc - s_loc) < c * KB
        tile(2, kw_ref[pl.ds(k0, KB), :], vwt_ref[kc_idx], mask, False)
        return carry

    lax.fori_loop(jnp.maximum(0, n_win - i), n_win, win_body, 0)

    gt_ref[...] = jax.nn.sigmoid(gate_ref[0].T)
    for r in range(NSA_GROUP):
        g_cmp = gt_ref[3 * r:3 * r + 1, :]
        g_slc = gt_ref[3 * r + 1:3 * r + 2, :]
        g_win = gt_ref[3 * r + 2:3 * r + 3, :]
        o = (g_cmp * acc_ref[0, r]
             + (g_slc / l_ref[NSA_GROUP + r]) * acc_ref[1, r]
             + (g_win / l_ref[2 * NSA_GROUP + r]) * acc_ref[2, r])
        o_ref[:, r * LANES:(r + 1) * LANES] = o.T.astype(o_ref.dtype)


def _nsa_attention(proj_a, vt_chunks, kc, vct, proj_b, batch, seq):
    nq = seq // QB
    n_chunks = seq // KB
    gw = NSA_GROUP * LANES
    grid = (batch, NSA_KV_HEADS, nq)
    return pl.pallas_call(
        _nsa_kernel,
        grid=grid,
        in_specs=[
            pl.BlockSpec((QB, gw), lambda b, g, i: (b * nq + i, g)),
            pl.BlockSpec((seq, LANES), lambda b, g, i: (b, CB_KSLC + g)),
            pl.BlockSpec((seq, LANES), lambda b, g, i: (b, CB_KWIN + g)),
            pl.BlockSpec((n_chunks, LANES, KB), lambda b, g, i: (b * 4 + g, 0, 0)),
            pl.BlockSpec((n_chunks, LANES, KB), lambda b, g, i: (b * 4 + 2 + g, 0, 0)),
            pl.BlockSpec((1, NB_PAD, LANES), lambda b, g, i: (g * batch + b, 0, 0)),
            pl.BlockSpec((1, LANES, NB_PAD), lambda b, g, i: (g * batch + b, 0, 0)),
            pl.BlockSpec((1, QB, LANES), lambda b, g, i: (4 + g, b * nq + i, 0)),
        ],
        out_specs=pl.BlockSpec((QB, gw), lambda b, g, i: (b * nq + i, g)),
        out_shape=jax.ShapeDtypeStruct((batch * seq, NSA_HEADS * LANES), BF16),
        scratch_shapes=[
            pltpu.VMEM((NB_PAD, QB), F32),
            pltpu.VMEM((NB_PAD, QB), F32),
            pltpu.VMEM((LANES, QB), F32),
            pltpu.VMEM((3 * NSA_GROUP, 1, QB), F32),
            pltpu.VMEM((3 * NSA_GROUP, 1, QB), F32),
            pltpu.VMEM((3, NSA_GROUP, LANES, QB), F32),
        ],
        compiler_params=_params(("parallel", "parallel", "arbitrary")),
        name="nsa_attention",
    )(proj_a, proj_a, proj_a, vt_chunks, vt_chunks, kc, vct, proj_b)


def _dil_kernel(q0_ref, q1_ref, q2_ref, k0_ref, k1_ref, k2_ref, v0_ref, v1_ref, v2_ref,
                o0_ref, o1_ref, o2_ref):
    i = pl.program_id(2)
    row = lax.broadcasted_iota(jnp.int32, (QB, KB), 0)
    col = lax.broadcasted_iota(jnp.int32, (QB, KB), 1)
    delta = row - col
    qs = (q0_ref, q1_ref, q2_ref)
    ks = (k0_ref, k1_ref, k2_ref)
    vs = (v0_ref, v1_ref, v2_ref)
    outs = (o0_ref, o1_ref, o2_ref)
    q_start = pl.multiple_of(i * QB, QB)

    accs, lses = [], []
    for p, (window, dil) in enumerate(DIL_PAIRS):
        q = qs[p][...]
        k_ref, v_ref = ks[p], vs[p]
        on_grid = (delta & (dil - 1)) == 0

        s = jnp.where(on_grid & (delta >= 0), _nt_dot(q, k_ref[pl.ds(q_start, KB), :]) * SCALE, NEG)
        m = jnp.max(s, axis=1, keepdims=True)
        e = jnp.exp(s - m)
        l = jnp.sum(e, axis=1, keepdims=True)
        acc = jnp.dot(e.astype(BF16), v_ref[pl.ds(q_start, KB), :], preferred_element_type=F32)

        n_past = window // KB

        def body(c, carry, k_ref=k_ref, v_ref=v_ref, on_grid=on_grid, q=q, n_past=n_past):
            m, l, acc = carry
            k_start = pl.multiple_of((i - n_past + c) * KB, KB)
            ok = on_grid & (delta <= c * KB)
            s = jnp.where(ok, _nt_dot(q, k_ref[pl.ds(k_start, KB), :]) * SCALE, NEG)
            m_new = jnp.maximum(m, jnp.max(s, axis=1, keepdims=True))
            alpha = jnp.exp(m - m_new)
            e = jnp.exp(s - m_new)
            l = alpha * l + jnp.sum(e, axis=1, keepdims=True)
            acc = alpha * acc + jnp.dot(e.astype(BF16), v_ref[pl.ds(k_start, KB), :],
                                        preferred_element_type=F32)
            return m_new, l, acc

        m, l, acc = lax.fori_loop(jnp.maximum(0, n_past - i), n_past, body, (m, l, acc))
        accs.append(acc / l)
        lses.append(m + jnp.log(l))

    top = jnp.maximum(jnp.maximum(lses[0], lses[1]), lses[2])
    ws = [jnp.exp(x - top) for x in lses]
    inv = 1.0 / (ws[0] + ws[1] + ws[2])
    for p in range(len(DIL_PAIRS)):
        outs[p][...] = (accs[p] * (ws[p] * inv)).astype(outs[p].dtype)


def _dil_attention(proj_a, batch, seq):
    nq = seq // QB
    hp = DIL_HEADS_PER_PAIR

    def q_spec(p):
        return pl.BlockSpec((QB, LANES), lambda b, h, i: (b * nq + i, CB_DQ + p * hp + h))

    def kv_spec(base, p):
        return pl.BlockSpec((seq, LANES), lambda b, h, i: (b, base + p * hp + h))

    out_spec = pl.BlockSpec((QB, LANES), lambda b, h, i: (b * nq + i, h))
    out_shape = jax.ShapeDtypeStruct((batch * seq, hp * LANES), BF16)
    return pl.pallas_call(
        _dil_kernel,
        grid=(batch, hp, nq),
        in_specs=[q_spec(0), q_spec(1), q_spec(2),
                  kv_spec(CB_DK, 0), kv_spec(CB_DK, 1), kv_spec(CB_DK, 2),
                  kv_spec(CB_DV, 0), kv_spec(CB_DV, 1), kv_spec(CB_DV, 2)],
        out_specs=[out_spec, out_spec, out_spec],
        out_shape=[out_shape, out_shape, out_shape],
        compiler_params=_params(("parallel", "parallel", "arbitrary")),
        name="dilated_attention",
    )(*([proj_a] * 9))


def _sb_kernel(q_ref, k_ref, v_ref, o_ref):
    i = pl.program_id(2)
    row = lax.broadcasted_iota(jnp.int32, (QB, KB), 0)
    col = lax.broadcasted_iota(jnp.int32, (QB, KB), 1)
    before = col < row
    suffix = jnp.where(row > col, 1.0, 0.0).astype(BF16)
    q = q_ref[...]

    def tile(k_chunk, v_chunk, carry_sum, diag):
        z = _nt_dot(q, k_chunk) * SCALE
        soft = jnp.log1p(jnp.exp(-jnp.abs(z)))
        log_beta = jnp.minimum(z, 0.0) - soft
        log_1m = -jnp.maximum(z, 0.0) - soft
        if diag:
            log_1m = jnp.where(before, log_1m, 0.0)
        hi = log_1m.astype(BF16)
        lo = (log_1m - hi.astype(F32)).astype(BF16)
        keep = (jnp.dot(hi, suffix, preferred_element_type=F32)
                + jnp.dot(lo, suffix, preferred_element_type=F32))
        a = jnp.exp(log_beta + keep + carry_sum)
        if diag:
            a = jnp.where(before, a, 0.0)
        contrib = jnp.dot(a.astype(BF16), v_chunk, preferred_element_type=F32)
        return contrib, carry_sum + jnp.sum(log_1m, axis=1, keepdims=True)

    q_start = pl.multiple_of(i * QB, QB)
    acc, csum = tile(k_ref[pl.ds(q_start, KB), :], v_ref[pl.ds(q_start, KB), :],
                     jnp.zeros((QB, 1), F32), True)

    def body(c, carry):
        acc, csum = carry
        k_start = pl.multiple_of((i - 1 - c) * KB, KB)
        contrib, csum = tile(k_ref[pl.ds(k_start, KB), :], v_ref[pl.ds(k_start, KB), :], csum, False)
        return acc + contrib, csum

    acc, _ = lax.fori_loop(0, i, body, (acc, csum))
    o_ref[...] = acc.astype(o_ref.dtype)


def _sb_attention(proj_a, batch, seq):
    nq = seq // QB
    return pl.pallas_call(
        _sb_kernel,
        grid=(batch, SB_HEADS, nq),
        in_specs=[pl.BlockSpec((QB, LANES), lambda b, h, i: (b * nq + i, CB_SQ + h)),
                  pl.BlockSpec((seq, LANES), lambda b, h, i: (b, CB_SK + h)),
                  pl.BlockSpec((seq, LANES), lambda b, h, i: (b, CB_SV + h))],
        out_specs=pl.BlockSpec((QB, LANES), lambda b, h, i: (b * nq + i, h)),
        out_shape=jax.ShapeDtypeStruct((batch * seq, SB_HEADS * LANES), BF16),
        compiler_params=_params(("parallel", "parallel", "arbitrary")),
        name="stick_breaking_attention",
    )(proj_a, proj_a, proj_a)


OUT_PIECES = (NSA_HEADS * LANES,) + (DIL_HEADS_PER_PAIR * LANES,) * 3 + (SB_HEADS * LANES,)


def _out_proj_kernel(a0, a1, a2, a3, a4, w_ref, x_ref, o_ref):
    acc = x_ref[...]
    off = 0
    for a_ref, width in zip((a0, a1, a2, a3, a4), OUT_PIECES):
        acc = acc + jnp.dot(a_ref[...], w_ref[off:off + width, :], preferred_element_type=F32)
        off += width
    o_ref[...] = acc


def _out_proj(pieces, w, x2d):
    m, n = x2d.shape
    tm, tn = 1024, 512
    a_specs = [pl.BlockSpec((tm, width), lambda i, j: (i, 0)) for width in OUT_PIECES]
    return pl.pallas_call(
        _out_proj_kernel,
        grid=(m // tm, n // tn),
        in_specs=a_specs + [pl.BlockSpec((w.shape[0], tn), lambda i, j: (0, j)),
                            pl.BlockSpec((tm, tn), lambda i, j: (i, j))],
        out_specs=pl.BlockSpec((tm, tn), lambda i, j: (i, j)),
        out_shape=jax.ShapeDtypeStruct((m, n), F32),
        compiler_params=_params(("parallel", "arbitrary")),
        name="out_proj",
    )(*pieces, w, x2d)


def _mlp_up_kernel(a_ref, w_ref, o_ref):
    acc = jnp.dot(a_ref[...], w_ref[...], preferred_element_type=F32)
    r = jnp.maximum(acc, 0.0)
    o_ref[...] = (r * r).astype(o_ref.dtype)


def _mlp_up(h, w):
    m, k = h.shape
    n = w.shape[1]
    tm, tn = 1024, 512
    return pl.pallas_call(
        _mlp_up_kernel,
        grid=(m // tm, n // tn),
        in_specs=[pl.BlockSpec((tm, k), lambda i, j: (i, 0)),
                  pl.BlockSpec((k, tn), lambda i, j: (0, j))],
        out_specs=pl.BlockSpec((tm, tn), lambda i, j: (i, j)),
        out_shape=jax.ShapeDtypeStruct((m, n), BF16),
        compiler_params=_params(("parallel", "arbitrary")),
        name="mlp_up",
    )(h, w)


def _mlp_down_kernel(a_ref, w_ref, x_ref, o_ref, acc_ref, *, nk):
    k = pl.program_id(2)
    part = jnp.dot(a_ref[...], w_ref[...], preferred_element_type=F32)

    @pl.when(k == 0)
    def _():
        acc_ref[...] = x_ref[...] + part

    @pl.when(k > 0)
    def _():
        acc_ref[...] += part

    @pl.when(k == nk - 1)
    def _():
        o_ref[...] = acc_ref[...]


def _mlp_down(a, w, x2d):
    m, kdim = a.shape
    n = w.shape[1]
    tm, tn, tk = 1024, 1024, 2048
    nk = kdim // tk
    return pl.pallas_call(
        functools.partial(_mlp_down_kernel, nk=nk),
        grid=(m // tm, n // tn, nk),
        in_specs=[pl.BlockSpec((tm, tk), lambda i, j, k: (i, k)),
                  pl.BlockSpec((tk, tn), lambda i, j, k: (k, j)),
                  pl.BlockSpec((tm, tn), lambda i, j, k: (i, j))],
        out_specs=pl.BlockSpec((tm, tn), lambda i, j, k: (i, j)),
        out_shape=jax.ShapeDtypeStruct((m, n), F32),
        scratch_shapes=[pltpu.VMEM((tm, tn), F32)],
        compiler_params=_params(("parallel", "parallel", "arbitrary")),
        name="mlp_down",
    )(a, w, x2d)


def _rope_tables(seq):
    inv = 1.0 / (ROPE_THETA ** (jnp.arange(0, HEAD_DIM, 2, dtype=F32) / HEAD_DIM))
    ang = jnp.arange(seq, dtype=F32)[:, None] * inv[None, :]
    cos, sin = jnp.cos(ang), jnp.sin(ang)
    return jnp.concatenate([cos, cos], axis=-1), jnp.concatenate([-sin, sin], axis=-1)


def _split_w_in(w_in):
    kvw = NSA_KV_HEADS * HEAD_DIM
    o_q = 0
    o_kc = NSA_HEADS * HEAD_DIM
    o_vc, o_ks, o_vs, o_kw, o_vw = (o_kc + n * kvw for n in range(1, 6))
    o_gate = o_kc + 6 * kvw
    n_gate = NSA_HEADS * 3
    o_dil = o_gate + n_gate
    cols = lambda a, n: w_in[:, a:a + n]
    w_a = jnp.concatenate(
        [cols(o_q, o_kc), cols(o_ks, kvw), cols(o_kw, kvw), cols(o_vs, kvw), cols(o_vw, kvw),
         w_in[:, o_dil:]], axis=1).astype(BF16)
    gate_pad = jnp.zeros((w_in.shape[0], LANES - n_gate // 2), w_in.dtype)
    w_b = jnp.concatenate(
        [cols(o_kc, kvw), cols(o_vc, kvw),
         cols(o_gate, n_gate // 2), gate_pad, cols(o_gate + n_gate // 2, n_gate // 2), gate_pad],
        axis=1).astype(BF16)
    return w_a, w_b


def _layer(x2d, batch, seq, cosf, sinf, w_in, g_mix, w_out, g_mlp, w_up, w_down,
           pe_k, pe_v, w_ck1, w_ck2, w_cv1, w_cv2):
    nb = seq // NSA_BLOCK
    groups = NSA_KV_HEADS
    w_a, w_b = _split_w_in(w_in)

    h = _rmsnorm(x2d, g_mix, BF16)
    proj_a = _proj_a(h, w_a, cosf, sinf, seq)
    proj_b = _proj_b(h, w_b, cosf, sinf, seq)

    x_kv = proj_b[:2 * groups].reshape(2 * groups * batch * nb, NSA_BLOCK * HEAD_DIM)
    pe = jnp.stack([pe_k.reshape(1, -1), pe_v.reshape(1, -1)])
    w1 = jnp.stack([w_ck1, w_cv1]).astype(BF16)
    w2 = jnp.stack([w_ck2, w_cv2]).astype(BF16)
    cmp_kv = _compress(x_kv, pe, w1, w2)
    cmp_kv = cmp_kv.reshape(2, groups * batch, nb, HEAD_DIM)
    cmp_kv = jnp.pad(cmp_kv, ((0, 0), (0, 0), (0, NB_PAD - nb), (0, 0))).astype(BF16)
    kc = cmp_kv[0]
    vct = cmp_kv[1].transpose(0, 2, 1)

    v_nsa = proj_a[:, CB_VSLC * LANES:(CB_VSLC + 4) * LANES]
    vt_chunks = v_nsa.reshape(batch, seq // KB, KB, 4, HEAD_DIM).transpose(0, 3, 1, 4, 2)
    vt_chunks = vt_chunks.reshape(batch * 4 * (seq // KB), HEAD_DIM, KB)

    o_a = _nsa_attention(proj_a, vt_chunks, kc, vct, proj_b, batch, seq)
    o_b = _dil_attention(proj_a, batch, seq)
    o_c = _sb_attention(proj_a, batch, seq)

    x2d = _out_proj([o_a, *o_b, o_c], w_out.astype(BF16), x2d)
    h2 = _rmsnorm(x2d, g_mlp, BF16)
    up = _mlp_up(h2, w_up.astype(BF16))
    return _mlp_down(up, w_down.astype(BF16), x2d)


def kernel(x, w_in, w_out, norm_mix, norm_mlp, w_up, w_down, nsa_pe_k, nsa_pe_v,
           nsa_w_ck1, nsa_w_ck2, nsa_w_cv1, nsa_w_cv2, final_norm):
    batch, seq, d = x.shape
    cosf, sinf = _rope_tables(seq)
    x2d = x.reshape(batch * seq, d)
    for layer in range(w_in.shape[0]):
        x2d = _layer(x2d, batch, seq, cosf, sinf, w_in[layer], norm_mix[layer], w_out[layer],
                     norm_mlp[layer], w_up[layer], w_down[layer], nsa_pe_k[layer], nsa_pe_v[layer],
                     nsa_w_ck1[layer], nsa_w_ck2[layer], nsa_w_cv1[layer], nsa_w_cv2[layer])
    return _rmsnorm(x2d, final_norm, x.dtype).reshape(batch, seq, d)
```

```python
import functools

import jax
import jax.numpy as jnp
from jax import lax
from jax.experimental import pallas as pl
from jax.experimental.pallas import tpu as pltpu

F32 = jnp.float32
BF16 = jnp.bfloat16

D_MODEL = 4096
HEAD_DIM = 128
ROPE_THETA = 10000.0
EPS = 1e-6
NEG = -1e30
NSA_HEADS = 12
NSA_KV_HEADS = 2
NSA_GROUP = NSA_HEADS // NSA_KV_HEADS
NSA_BLOCK = 64
NSA_TOPN = 16
NSA_WINDOW = 512
NSA_FORCED_SCORE = NSA_GROUP + 1.0
DIL_PAIRS = ((128, 1), (512, 4), (2048, 16))
DIL_HEADS_PER_PAIR = 4
DIL_HEADS = DIL_HEADS_PER_PAIR * len(DIL_PAIRS)
SB_HEADS = 8
D_FF = 4 * D_MODEL
SCALE = HEAD_DIM ** -0.5

LANES = 128
VMEM_LIMIT = 56 * 1024 * 1024

KV_W = NSA_KV_HEADS * HEAD_DIM
W_KCMP = NSA_HEADS * HEAD_DIM
W_GATE = W_KCMP + 6 * KV_W
N_GATE = NSA_HEADS * 3
W_TAIL = W_GATE + N_GATE
PROJ_TN = 512
N_COLS = 20 * LANES
CB_NSA_Q, CB_KSLC, CB_VSLC, CB_KWIN, CB_VWIN = 0, 12, 14, 16, 18
A_COLS = 60 * LANES
CB_DQ, CB_DK, CB_DV = 0, 12, 24
CB_SQ, CB_SK, CB_SV = 36, 44, 52
B_CHUNKS = 5
B_GATE = 4


def _params(sem):
    return pltpu.CompilerParams(dimension_semantics=sem, vmem_limit_bytes=VMEM_LIMIT)


def _nt_dot(a, b):
    return lax.dot_general(a, b, (((1,), (1,)), ((), ())), preferred_element_type=F32)


def _dot(a, b):
    return jnp.dot(a, b, preferred_element_type=F32)


def _rmsnorm_kernel(x_ref, g_ref, o_ref):
    x = x_ref[...]
    ms = jnp.mean(x * x, axis=-1, keepdims=True)
    o_ref[...] = (x * lax.rsqrt(ms + EPS) * g_ref[...]).astype(o_ref.dtype)


def _rmsnorm(x2d, g, out_dtype):
    m, d = x2d.shape
    tm = 256
    return pl.pallas_call(
        _rmsnorm_kernel,
        grid=(m // tm,),
        in_specs=[pl.BlockSpec((tm, d), lambda i: (i, 0)),
                  pl.BlockSpec((1, d), lambda i: (0, 0))],
        out_specs=pl.BlockSpec((tm, d), lambda i: (i, 0)),
        out_shape=jax.ShapeDtypeStruct((m, d), out_dtype),
        compiler_params=_params(("parallel",)),
        name="rmsnorm",
    )(x2d, g.reshape(1, d))


def _rope(xc, cos, sin):
    return xc * cos + pltpu.roll(xc, HEAD_DIM // 2, 1) * sin


def _proj_kernel(a_ref, w_ref, cos_ref, sin_ref, o_ref, *, n_full, n_half, n_tiles):
    j = pl.program_id(1)
    acc = _dot(a_ref[...], w_ref[0])

    def store(n_roped):
        for c in range(PROJ_TN // LANES):
            sl = slice(c * LANES, (c + 1) * LANES)
            xc = acc[:, sl]
            if c < n_roped:
                xc = _rope(xc, cos_ref[...], sin_ref[...])
            o_ref[:, sl] = xc.astype(o_ref.dtype)

    pl.when(j < n_full)(lambda: store(PROJ_TN // LANES))
    if n_half:
        pl.when((j >= n_full) & (j < n_full + n_half))(lambda: store(NSA_KV_HEADS))
    if n_full + n_half < n_tiles:
        pl.when(j >= n_full + n_half)(lambda: store(0))


def _proj(h, w3d, layer, src_tile, n_tiles, n_full, n_half, cosf, sinf, seq):
    m, k = h.shape
    tm = 1024
    t_blocks = seq // tm
    return pl.pallas_call(
        functools.partial(_proj_kernel, n_full=n_full, n_half=n_half, n_tiles=n_tiles),
        grid=(m // tm, n_tiles),
        in_specs=[pl.BlockSpec((tm, k), lambda i, j: (i, 0)),
                  pl.BlockSpec((1, k, PROJ_TN), lambda i, j: (layer, 0, src_tile(j))),
                  pl.BlockSpec((tm, LANES), lambda i, j: (i % t_blocks, 0)),
                  pl.BlockSpec((tm, LANES), lambda i, j: (i % t_blocks, 0))],
        out_specs=pl.BlockSpec((tm, PROJ_TN), lambda i, j: (i, j)),
        out_shape=jax.ShapeDtypeStruct((m, n_tiles * PROJ_TN), BF16),
        compiler_params=_params(("parallel", "arbitrary")),
        name="proj",
    )(h, w3d, cosf, sinf)


def _proj_b_kernel(a_ref, w_ref, wg_ref, cos_ref, sin_ref, o_ref):
    a = a_ref[...]
    acc = _dot(a, w_ref[0])
    for c in range(B_GATE):
        xc = acc[:, c * LANES:(c + 1) * LANES]
        o_ref[c] = _rope(xc, cos_ref[...], sin_ref[...]) if c < NSA_KV_HEADS else xc
    o_ref[B_GATE] = _dot(a, wg_ref[0])


def _proj_b(h, w3d, layer, cosf, sinf, seq):
    m, k = h.shape
    tm = 1024
    t_blocks = seq // tm
    return pl.pallas_call(
        _proj_b_kernel,
        grid=(m // tm,),
        in_specs=[pl.BlockSpec((tm, k), lambda i: (i, 0)),
                  pl.BlockSpec((1, k, 2 * KV_W), lambda i: (layer, 0, W_KCMP // (2 * KV_W))),
                  pl.BlockSpec((1, k, LANES), lambda i: (layer, 0, W_GATE // LANES)),
                  pl.BlockSpec((tm, LANES), lambda i: (i % t_blocks, 0)),
                  pl.BlockSpec((tm, LANES), lambda i: (i % t_blocks, 0))],
        out_specs=pl.BlockSpec((B_CHUNKS, tm, LANES), lambda i: (0, i, 0)),
        out_shape=jax.ShapeDtypeStruct((B_CHUNKS, m, LANES), F32),
        compiler_params=_params(("parallel",)),
        name="proj_b",
    )(h, w3d, w3d, cosf, sinf)


def _compress_kernel(x_ref, pe_ref, w1_ref, w2_ref, o_ref):
    blk = (x_ref[...] + pe_ref[0]).astype(BF16)
    hid = _dot(blk, w1_ref[0])
    hid = hid * jax.nn.sigmoid(hid)
    o_ref[0] = _dot(hid.astype(BF16), w2_ref[0])


def _compress(x_kv, pe, w1, w2):
    rows = x_kv.shape[0] // 2
    kdim = x_kv.shape[1]
    hidden = w1.shape[-1]
    return pl.pallas_call(
        _compress_kernel,
        grid=(2,),
        in_specs=[pl.BlockSpec((rows, kdim), lambda i: (i, 0)),
                  pl.BlockSpec((1, 1, kdim), lambda i: (i, 0, 0)),
                  pl.BlockSpec((1, kdim, hidden), lambda i: (i, 0, 0)),
                  pl.BlockSpec((1, hidden, HEAD_DIM), lambda i: (i, 0, 0))],
        out_specs=pl.BlockSpec((1, rows, HEAD_DIM), lambda i: (i, 0, 0)),
        out_shape=jax.ShapeDtypeStruct((2, rows, HEAD_DIM), F32),
        compiler_params=_params(("parallel",)),
        name="nsa_compress",
    )(x_kv, pe, w1, w2)


NSA_QB = 128
NSA_KC = 128
NSA_CW = 512
NSA_WIN_KEYS = NSA_WINDOW + NSA_QB
NB_PAD = 128


def _nsa_kernel(q_ref, ks_ref, kw_ref, vs_ref, vw_ref, kc_ref, vct_ref, gate_ref, o_ref,
                vst_ref, vwt_ref, val_ref, sel_ref, gt_ref, acc_ref, *, seq):
    i = pl.program_id(2)
    qb = NSA_QB
    n_blocks = seq // NSA_BLOCK
    q0 = pl.multiple_of(i * qb, qb)

    @pl.when(i == 0)
    def _():
        for c in range(seq // NSA_KC):
            rows = slice(c * NSA_KC, (c + 1) * NSA_KC)
            vst_ref[c] = vs_ref[rows, :].astype(F32).T.astype(BF16)
            vwt_ref[c] = vw_ref[rows, :].astype(F32).T.astype(BF16)

    qs = [q_ref[:, r * LANES:(r + 1) * LANES] for r in range(NSA_GROUP)]
    t_row = i * qb + lax.broadcasted_iota(jnp.int32, (1, qb), 1)
    sub_n = lax.broadcasted_iota(jnp.int32, (NB_PAD, qb), 0)
    t_full = i * qb + lax.broadcasted_iota(jnp.int32, (NB_PAD, qb), 1)

    cmp_ok = (sub_n * NSA_BLOCK + (NSA_BLOCK - 1) <= t_full) & (sub_n < n_blocks)
    any_ok = jnp.where(t_row >= NSA_BLOCK - 1, 1.0, 0.0)
    kc = kc_ref[0]
    vct = vct_ref[0]
    imp = jnp.zeros((NB_PAD, qb), F32)
    o_cmp = []
    for r in range(NSA_GROUP):
        s = jnp.where(cmp_ok, _nt_dot(kc, qs[r]) * SCALE, NEG)
        e = jnp.exp(s - jnp.max(s, axis=0, keepdims=True))
        p = e / jnp.sum(e, axis=0, keepdims=True) * any_ok
        imp = imp + p
        o_cmp.append(_dot(vct, p.astype(BF16)))

    cur = jnp.right_shift(t_full, 6)
    started = sub_n <= cur
    forced = (sub_n == 0) | (sub_n == cur) | (sub_n == cur - 1)
    val = jnp.where(forced, NSA_FORCED_SCORE, jnp.where(started, imp, -1.0))
    val_ref[...] = val
    val_n = val[:n_blocks]
    sub_nn = sub_n[:n_blocks]
    rank = jnp.zeros((n_blocks, qb), F32)
    for n2 in range(n_blocks):
        row = val_ref[n2:n2 + 1, :]
        rank = rank + jnp.where(sub_nn > n2, jnp.where(row >= val_n, 1.0, 0.0),
                                jnp.where(row > val_n, 1.0, 0.0))
    sel_ref[0:n_blocks, :] = jnp.where(rank < NSA_TOPN, 1.0, 0.0)

    per_step = NSA_CW // NSA_BLOCK
    d_cw = (lax.broadcasted_iota(jnp.int32, (NSA_CW, qb), 0)
            - lax.broadcasted_iota(jnp.int32, (NSA_CW, qb), 1))

    def slc_scores(cw):
        k0 = cw * NSA_CW if isinstance(cw, int) else pl.multiple_of(cw * NSA_CW, NSA_CW)
        k_chunk = ks_ref[pl.ds(k0, NSA_CW), :]
        picked = jnp.concatenate(
            [jnp.broadcast_to(sel_ref[pl.ds(per_step * cw + b, 1), :], (NSA_BLOCK, qb))
             for b in range(per_step)], axis=0)
        mask = jnp.where(d_cw <= q0 - k0, picked, 0.0) > 0.5
        return [jnp.where(mask, _nt_dot(k_chunk, qs[r]) * SCALE, NEG) for r in range(NSA_GROUP)]

    def pv(vt_ref, first_chunk, p, n_chunks):
        out = None
        for j in range(n_chunks):
            part = _dot(vt_ref[first_chunk + j], p[j * NSA_KC:(j + 1) * NSA_KC].astype(BF16))
            out = part if out is None else out + part
        return out

    ms, ls = [], []
    for r, s in enumerate(slc_scores(0)):
        m = jnp.max(s, axis=0, keepdims=True)
        p = jnp.exp(s - m)
        ms.append(m)
        ls.append(jnp.sum(p, axis=0, keepdims=True))
        acc_ref[r] = pv(vst_ref, 0, p, NSA_CW // NSA_KC)

    def slc_body(cw, carry):
        ms, ls = carry
        new_m, new_l = [], []
        for r, s in enumerate(slc_scores(cw)):
            m_new = jnp.maximum(ms[r], jnp.max(s, axis=0, keepdims=True))
            alpha = jnp.exp(ms[r] - m_new)
            p = jnp.exp(s - m_new)
            new_m.append(m_new)
            new_l.append(alpha * ls[r] + jnp.sum(p, axis=0, keepdims=True))
            acc_ref[r] = alpha * acc_ref[r] + pv(vst_ref, cw * (NSA_CW // NSA_KC), p, NSA_CW // NSA_KC)
        return tuple(new_m), tuple(new_l)

    n_steps = jnp.right_shift(q0 + (qb + NSA_CW - 1), NSA_CW.bit_length() - 1)
    _, ls = lax.fori_loop(1, n_steps, slc_body, (tuple(ms), tuple(ls)))

    w_start = pl.multiple_of(jnp.maximum(q0 - NSA_WINDOW, 0), NSA_KC)
    delta = ((q0 - w_start) + lax.broadcasted_iota(jnp.int32, (NSA_WIN_KEYS, qb), 1)
             - lax.broadcasted_iota(jnp.int32, (NSA_WIN_KEYS, qb), 0))
    in_win = (delta >= 0) & (delta < NSA_WINDOW)
    k_win = kw_ref[pl.ds(w_start, NSA_WIN_KEYS), :]
    first_chunk = jnp.right_shift(w_start, NSA_KC.bit_length() - 1)

    gt_ref[...] = jax.nn.sigmoid(gate_ref[0].T)
    g_row = pl.program_id(1) * (NSA_GROUP * 3)
    for r in range(NSA_GROUP):
        s = jnp.where(in_win, _nt_dot(k_win, qs[r]) * SCALE, NEG)
        p = jnp.exp(s - jnp.max(s, axis=0, keepdims=True))
        l_win = jnp.sum(p, axis=0, keepdims=True)
        o_win = pv(vwt_ref, first_chunk, p, NSA_WIN_KEYS // NSA_KC)
        g_cmp = gt_ref[pl.ds(g_row + 3 * r, 1), :]
        g_slc = gt_ref[pl.ds(g_row + 3 * r + 1, 1), :]
        g_win = gt_ref[pl.ds(g_row + 3 * r + 2, 1), :]
        o = g_cmp * o_cmp[r] + (g_slc / ls[r]) * acc_ref[r] + (g_win / l_win) * o_win
        o_ref[:, r * LANES:(r + 1) * LANES] = o.T.astype(o_ref.dtype)


def _nsa_attention(proj_n, kc, vct, proj_b, batch, seq):
    qb = NSA_QB
    nq = seq // qb
    n_chunks = seq // NSA_KC
    gw = NSA_GROUP * LANES
    kv_spec = lambda base: pl.BlockSpec((seq, LANES), lambda b, g, i: (b, base + g))
    return pl.pallas_call(
        functools.partial(_nsa_kernel, seq=seq),
        grid=(batch, NSA_KV_HEADS, nq),
        in_specs=[
            pl.BlockSpec((qb, gw), lambda b, g, i: (b * nq + i, g)),
            kv_spec(CB_KSLC), kv_spec(CB_KWIN), kv_spec(CB_VSLC), kv_spec(CB_VWIN),
            pl.BlockSpec((1, NB_PAD, LANES), lambda b, g, i: (g * batch + b, 0, 0)),
            pl.BlockSpec((1, LANES, NB_PAD), lambda b, g, i: (g * batch + b, 0, 0)),
            pl.BlockSpec((1, qb, LANES), lambda b, g, i: (B_GATE, b * nq + i, 0)),
        ],
        out_specs=pl.BlockSpec((qb, gw), lambda b, g, i: (b * nq + i, g)),
        out_shape=jax.ShapeDtypeStruct((batch * seq, NSA_HEADS * LANES), BF16),
        scratch_shapes=[
            pltpu.VMEM((n_chunks, LANES, NSA_KC), BF16),
            pltpu.VMEM((n_chunks, LANES, NSA_KC), BF16),
            pltpu.VMEM((NB_PAD, qb), F32),
            pltpu.VMEM((NB_PAD, qb), F32),
            pltpu.VMEM((LANES, qb), F32),
            pltpu.VMEM((NSA_GROUP, LANES, qb), F32),
        ],
        compiler_params=_params(("arbitrary", "arbitrary", "arbitrary")),
        name="nsa_attention",
    )(proj_n, proj_n, proj_n, proj_n, proj_n, kc, vct, proj_b)


DIL_QB = 256


def _dil_bias(window, dil):
    rows = jnp.arange(DIL_QB, dtype=jnp.int32)[:, None]
    cols = jnp.arange(window + DIL_QB, dtype=jnp.int32)[None, :]
    delta = window + rows - cols
    ok = (delta >= 0) & (delta <= window) & (delta % dil == 0)
    return jnp.where(ok, 0.0, NEG).astype(F32)


def _dil_kernel(q0_ref, q1_ref, q2_ref, k0_ref, k1_ref, k2_ref, v0_ref, v1_ref, v2_ref,
                b0_ref, b1_ref, b2_ref, o0_ref, o1_ref, o2_ref,
                kp0_ref, kp1_ref, kp2_ref, vp0_ref, vp1_ref, vp2_ref, *, seq):
    i = pl.program_id(2)
    qs = (q0_ref, q1_ref, q2_ref)
    ks = (k0_ref, k1_ref, k2_ref)
    vs = (v0_ref, v1_ref, v2_ref)
    biases = (b0_ref, b1_ref, b2_ref)
    kps = (kp0_ref, kp1_ref, kp2_ref)
    vps = (vp0_ref, vp1_ref, vp2_ref)
    outs = (o0_ref, o1_ref, o2_ref)

    @pl.when(i == 0)
    def _():
        for p, (window, _) in enumerate(DIL_PAIRS):
            kps[p][0:window, :] = jnp.zeros((window, LANES), BF16)
            vps[p][0:window, :] = jnp.zeros((window, LANES), BF16)
            kps[p][window:window + seq, :] = ks[p][...]
            vps[p][window:window + seq, :] = vs[p][...]

    q_start = pl.multiple_of(i * DIL_QB, DIL_QB)
    accs, lses = [], []
    for p, (window, _) in enumerate(DIL_PAIRS):
        width = window + DIL_QB
        k_win = kps[p][pl.ds(q_start, width), :]
        v_win = vps[p][pl.ds(q_start, width), :]
        col = lax.broadcasted_iota(jnp.int32, (1, width), 1)
        in_seq = jnp.where(col >= window - q_start, 0.0, NEG)
        s = _nt_dot(qs[p][...], k_win) * SCALE + biases[p][...] + in_seq
        m = jnp.max(s, axis=1, keepdims=True)
        e = jnp.exp(s - m)
        l = jnp.sum(e, axis=1, keepdims=True)
        accs.append(_dot(e.astype(BF16), v_win) / l)
        lses.append(m + jnp.log(l))

    top = jnp.maximum(jnp.maximum(lses[0], lses[1]), lses[2])
    ws = [jnp.exp(x - top) for x in lses]
    inv = 1.0 / (ws[0] + ws[1] + ws[2])
    for p in range(len(DIL_PAIRS)):
        outs[p][...] = (accs[p] * (ws[p] * inv)).astype(outs[p].dtype)


def _dil_attention(proj_d, batch, seq):
    nq = seq // DIL_QB
    hp = DIL_HEADS_PER_PAIR

    def q_spec(p):
        return pl.BlockSpec((DIL_QB, LANES), lambda b, h, i: (b * nq + i, CB_DQ + p * hp + h))

    def kv_spec(base, p):
        return pl.BlockSpec((seq, LANES), lambda b, h, i: (b, base + p * hp + h))

    def bias_spec(p):
        return pl.BlockSpec((DIL_QB, DIL_PAIRS[p][0] + DIL_QB), lambda b, h, i: (0, 0))

    out_spec = pl.BlockSpec((DIL_QB, LANES), lambda b, h, i: (b * nq + i, h))
    out_shape = jax.ShapeDtypeStruct((batch * seq, hp * LANES), BF16)
    staged = [pltpu.VMEM((w + seq, LANES), BF16) for w, _ in DIL_PAIRS]
    return pl.pallas_call(
        functools.partial(_dil_kernel, seq=seq),
        grid=(batch, hp, nq),
        in_specs=[q_spec(0), q_spec(1), q_spec(2),
                  kv_spec(CB_DK, 0), kv_spec(CB_DK, 1), kv_spec(CB_DK, 2),
                  kv_spec(CB_DV, 0), kv_spec(CB_DV, 1), kv_spec(CB_DV, 2),
                  bias_spec(0), bias_spec(1), bias_spec(2)],
        out_specs=[out_spec, out_spec, out_spec],
        out_shape=[out_shape, out_shape, out_shape],
        scratch_shapes=staged + staged,
        compiler_params=_params(("arbitrary", "arbitrary", "arbitrary")),
        name="dilated_attention",
    )(*([proj_d] * 9), *[_dil_bias(w, d) for w, d in DIL_PAIRS])


SB_QB = 256
SB_PAST = 256
SB_KB = 128
SB_DONE = -104.0


def _sb_suffix(n):
    j = jnp.arange(n, dtype=jnp.int32)[:, None]
    s = jnp.arange(n, dtype=jnp.int32)[None, :]
    return jnp.where(j > s, 1.0, 0.0).astype(BF16)


def _sb_kernel(q_ref, k_ref, v_ref, suffix_ref, o_ref):
    i = pl.program_id(2)
    width = SB_PAST + SB_QB
    q = q_ref[...]

    def tile(k_chunk, v_chunk, suffix, carry_sum, before):
        z = _nt_dot(q, k_chunk) * SCALE
        soft = jnp.log1p(jnp.exp(-jnp.abs(z)))
        log_beta = jnp.minimum(z, 0.0) - soft
        log_1m = -jnp.maximum(z, 0.0) - soft
        if before is not None:
            log_1m = jnp.where(before, log_1m, 0.0)
        hi = log_1m.astype(BF16)
        lo = (log_1m - hi.astype(F32)).astype(BF16)
        logit = log_beta + (_dot(hi, suffix) + _dot(lo, suffix)) + carry_sum
        if before is not None:
            logit = jnp.where(before, logit, NEG)
        contrib = _dot(jnp.exp(logit).astype(BF16), v_chunk)
        return contrib, carry_sum + jnp.sum(log_1m, axis=1, keepdims=True)

    q_start = i * SB_QB
    w_start = pl.multiple_of(jnp.maximum(q_start - SB_PAST, 0), SB_QB)
    offs = (lax.broadcasted_iota(jnp.int32, (SB_QB, width), 1)
            - lax.broadcasted_iota(jnp.int32, (SB_QB, width), 0))
    before = offs < q_start - w_start
    acc, csum = tile(k_ref[pl.ds(w_start, width), :], v_ref[pl.ds(w_start, width), :],
                     suffix_ref[...], jnp.zeros((SB_QB, 1), F32), before)

    n_rest = jnp.right_shift(w_start, SB_KB.bit_length() - 1)
    suffix_kb = suffix_ref[0:SB_KB, 0:SB_KB]

    def cond(carry):
        c, _, _, live = carry
        return (c < n_rest) & (live > SB_DONE)

    def body(carry):
        c, acc, csum, _ = carry
        k_start = pl.multiple_of((n_rest - 1 - c) * SB_KB, SB_KB)
        contrib, csum = tile(k_ref[pl.ds(k_start, SB_KB), :], v_ref[pl.ds(k_start, SB_KB), :],
                             suffix_kb, csum, None)
        return c + 1, acc + contrib, csum, jnp.max(csum)

    _, acc, _, _ = lax.while_loop(cond, body, (jnp.int32(0), acc, csum, jnp.max(csum)))
    o_ref[...] = acc.astype(o_ref.dtype)


def _sb_attention(proj_d, batch, seq):
    nq = seq // SB_QB
    width = SB_PAST + SB_QB
    return pl.pallas_call(
        _sb_kernel,
        grid=(batch, SB_HEADS, nq),
        in_specs=[pl.BlockSpec((SB_QB, LANES), lambda b, h, i: (b * nq + i, CB_SQ + h)),
                  pl.BlockSpec((seq, LANES), lambda b, h, i: (b, CB_SK + h)),
                  pl.BlockSpec((seq, LANES), lambda b, h, i: (b, CB_SV + h)),
                  pl.BlockSpec((width, width), lambda b, h, i: (0, 0))],
        out_specs=pl.BlockSpec((SB_QB, LANES), lambda b, h, i: (b * nq + i, h)),
        out_shape=jax.ShapeDtypeStruct((batch * seq, SB_HEADS * LANES), BF16),
        compiler_params=_params(("parallel", "parallel", "arbitrary")),
        name="stick_breaking_attention",
    )(proj_d, proj_d, proj_d, _sb_suffix(width))


OUT_PIECES = (NSA_HEADS * LANES,) + (DIL_HEADS_PER_PAIR * LANES,) * 3 + (SB_HEADS * LANES,)


def _out_proj_kernel(a0, a1, a2, a3, a4, w_ref, x_ref, o_ref):
    acc = x_ref[...]
    off = 0
    for a_ref, width in zip((a0, a1, a2, a3, a4), OUT_PIECES):
        acc = acc + _dot(a_ref[...], w_ref[0, off:off + width, :])
        off += width
    o_ref[...] = acc


def _out_proj(pieces, w3d, layer, x2d):
    m, n = x2d.shape
    tm, tn = 1024, 512
    a_specs = [pl.BlockSpec((tm, width), lambda i, j: (i, 0)) for width in OUT_PIECES]
    return pl.pallas_call(
        _out_proj_kernel,
        grid=(m // tm, n // tn),
        in_specs=a_specs + [pl.BlockSpec((1, w3d.shape[1], tn), lambda i, j: (layer, 0, j)),
                            pl.BlockSpec((tm, tn), lambda i, j: (i, j))],
        out_specs=pl.BlockSpec((tm, tn), lambda i, j: (i, j)),
        out_shape=jax.ShapeDtypeStruct((m, n), F32),
        compiler_params=_params(("parallel", "arbitrary")),
        name="out_proj",
    )(*pieces, w3d, x2d)


def _mlp_up_kernel(a_ref, w_ref, o_ref):
    r = jnp.maximum(_dot(a_ref[...], w_ref[0]), 0.0)
    o_ref[...] = (r * r).astype(o_ref.dtype)


def _mlp_up(h, w3d, layer):
    m, k = h.shape
    n = w3d.shape[2]
    tm, tn = 1024, 512
    return pl.pallas_call(
        _mlp_up_kernel,
        grid=(m // tm, n // tn),
        in_specs=[pl.BlockSpec((tm, k), lambda i, j: (i, 0)),
                  pl.BlockSpec((1, k, tn), lambda i, j: (layer, 0, j))],
        out_specs=pl.BlockSpec((tm, tn), lambda i, j: (i, j)),
        out_shape=jax.ShapeDtypeStruct((m, n), BF16),
        compiler_params=_params(("parallel", "arbitrary")),
        name="mlp_up",
    )(h, w3d)


def _mlp_down_kernel(a_ref, w_ref, x_ref, o_ref, acc_ref, *, nk):
    k = pl.program_id(2)
    part = _dot(a_ref[...], w_ref[0])

    @pl.when(k == 0)
    def _():
        acc_ref[...] = x_ref[...] + part

    @pl.when(k > 0)
    def _():
        acc_ref[...] += part

    @pl.when(k == nk - 1)
    def _():
        o_ref[...] = acc_ref[...]


def _mlp_down(a, w3d, layer, x2d):
    m, kdim = a.shape
    n = w3d.shape[2]
    tm, tn, tk = 1024, 1024, 2048
    nk = kdim // tk
    return pl.pallas_call(
        functools.partial(_mlp_down_kernel, nk=nk),
        grid=(m // tm, n // tn, nk),
        in_specs=[pl.BlockSpec((tm, tk), lambda i, j, k: (i, k)),
                  pl.BlockSpec((1, tk, tn), lambda i, j, k: (layer, k, j)),
                  pl.BlockSpec((tm, tn), lambda i, j, k: (i, j))],
        out_specs=pl.BlockSpec((tm, tn), lambda i, j, k: (i, j)),
        out_shape=jax.ShapeDtypeStruct((m, n), F32),
        scratch_shapes=[pltpu.VMEM((tm, tn), F32)],
        compiler_params=_params(("parallel", "parallel", "arbitrary")),
        name="mlp_down",
    )(a, w3d, x2d)


def _rope_tables(seq):
    inv = 1.0 / (ROPE_THETA ** (jnp.arange(0, HEAD_DIM, 2, dtype=F32) / HEAD_DIM))
    ang = jnp.arange(seq, dtype=F32)[:, None] * inv[None, :]
    cos, sin = jnp.cos(ang), jnp.sin(ang)
    return jnp.concatenate([cos, cos], axis=-1), jnp.concatenate([-sin, sin], axis=-1)


def _nsa_mixer(proj_n, proj_b, batch, seq, pe_k, pe_v, w_ck1, w_ck2, w_cv1, w_cv2):
    nb = seq // NSA_BLOCK
    groups = NSA_KV_HEADS
    x_kv = proj_b[:B_GATE].reshape(2 * groups * batch * nb, NSA_BLOCK * HEAD_DIM)
    pe = jnp.stack([pe_k.reshape(1, -1), pe_v.reshape(1, -1)])
    w1 = jnp.stack([w_ck1, w_cv1]).astype(BF16)
    w2 = jnp.stack([w_ck2, w_cv2]).astype(BF16)
    cmp_kv = _compress(x_kv, pe, w1, w2)
    cmp_kv = cmp_kv.reshape(2, groups * batch, nb, HEAD_DIM)
    cmp_kv = jnp.pad(cmp_kv, ((0, 0), (0, 0), (0, NB_PAD - nb), (0, 0))).astype(BF16)
    kc = cmp_kv[0]
    vct = cmp_kv[1].transpose(0, 2, 1)
    return _nsa_attention(proj_n, kc, vct, proj_b, batch, seq)


def _layer(x2d, batch, seq, cosf, sinf, layer, w_in, w_tail, g_mix, w_out, g_mlp, w_up, w_down,
           pe_k, pe_v, w_ck1, w_ck2, w_cv1, w_cv2):
    h = _rmsnorm(x2d, g_mix, BF16)
    proj_n = _proj(h, w_in, layer, lambda j: jnp.where(j >= 3, j + 1, j), N_COLS // PROJ_TN, 3, 2,
                   cosf, sinf, seq)
    proj_d = _proj(h, w_tail, layer, lambda j: j, A_COLS // PROJ_TN, 2 * DIL_HEADS * LANES // PROJ_TN, 0,
                   cosf, sinf, seq)
    proj_b = _proj_b(h, w_in, layer, cosf, sinf, seq)

    o_a = _nsa_mixer(proj_n, proj_b, batch, seq, pe_k, pe_v, w_ck1, w_ck2, w_cv1, w_cv2)
    o_b = _dil_attention(proj_d, batch, seq)
    o_c = _sb_attention(proj_d, batch, seq)

    x2d = _out_proj([o_a, *o_b, o_c], w_out, layer, x2d)
    h2 = _rmsnorm(x2d, g_mlp, BF16)
    up = _mlp_up(h2, w_up, layer)
    return _mlp_down(up, w_down, layer, x2d)


def kernel(x, w_in, w_out, norm_mix, norm_mlp, w_up, w_down, nsa_pe_k, nsa_pe_v,
           nsa_w_ck1, nsa_w_ck2, nsa_w_cv1, nsa_w_cv2, final_norm):
    batch, seq, d = x.shape
    cosf, sinf = _rope_tables(seq)
    w_in = w_in.astype(BF16)
    w_tail = w_in[:, :, W_TAIL:]
    w_out, w_up, w_down = w_out.astype(BF16), w_up.astype(BF16), w_down.astype(BF16)
    x2d = x.reshape(batch * seq, d)
    for layer in range(w_in.shape[0]):
        x2d = _layer(x2d, batch, seq, cosf, sinf, layer, w_in, w_tail, norm_mix[layer], w_out,
                     norm_mlp[layer], w_up, w_down, nsa_pe_k[layer], nsa_pe_v[layer],
                     nsa_w_ck1[layer], nsa_w_ck2[layer], nsa_w_cv1[layer], nsa_w_cv2[layer])
    return _rmsnorm(x2d, final_norm, x.dtype).reshape(batch, seq, d)
```

```python
import functools

import jax
import jax.numpy as jnp
from jax import lax
from jax.experimental import pallas as pl
from jax.experimental.pallas import tpu as pltpu

F32 = jnp.float32
BF16 = jnp.bfloat16

D_MODEL = 4096
HEAD_DIM = 128
ROPE_THETA = 10000.0
EPS = 1e-6
NEG = -1e30
NSA_HEADS = 12
NSA_KV_HEADS = 2
NSA_GROUP = NSA_HEADS // NSA_KV_HEADS
NSA_BLOCK = 64
NSA_TOPN = 16
NSA_WINDOW = 512
NSA_FORCED_SCORE = NSA_GROUP + 1.0
DIL_PAIRS = ((128, 1), (512, 4), (2048, 16))
DIL_HEADS_PER_PAIR = 4
DIL_HEADS = DIL_HEADS_PER_PAIR * len(DIL_PAIRS)
SB_HEADS = 8
D_FF = 4 * D_MODEL
SCALE = HEAD_DIM ** -0.5
SCALE_LOG2E = SCALE * 1.4426950408889634

LANES = 128
VMEM_LIMIT = 56 * 1024 * 1024

KV_W = NSA_KV_HEADS * HEAD_DIM
W_KCMP = NSA_HEADS * HEAD_DIM
W_GATE = W_KCMP + 6 * KV_W
N_GATE = NSA_HEADS * 3
W_TAIL = W_GATE + N_GATE
PROJ_TN = 512
N_COLS = 20 * LANES
CB_NSA_Q, CB_KSLC, CB_VSLC, CB_KWIN, CB_VWIN = 0, 12, 14, 16, 18
A_COLS = 60 * LANES
CB_DQ, CB_DK, CB_DV = 0, 12, 24
CB_SQ, CB_SK, CB_SV = 36, 44, 52
B_CHUNKS = 5
B_GATE = 4


def _params(sem):
    return pltpu.CompilerParams(dimension_semantics=sem, vmem_limit_bytes=VMEM_LIMIT)


def _nt_dot(a, b):
    return lax.dot_general(a, b, (((1,), (1,)), ((), ())), preferred_element_type=F32)


def _dot(a, b):
    return jnp.dot(a, b, preferred_element_type=F32)


def _rmsnorm_kernel(x_ref, g_ref, o_ref):
    x = x_ref[...]
    ms = jnp.mean(x * x, axis=-1, keepdims=True)
    o_ref[...] = (x * lax.rsqrt(ms + EPS) * g_ref[...]).astype(o_ref.dtype)


def _rmsnorm(x2d, g, out_dtype):
    m, d = x2d.shape
    tm = 256
    return pl.pallas_call(
        _rmsnorm_kernel,
        grid=(m // tm,),
        in_specs=[pl.BlockSpec((tm, d), lambda i: (i, 0)),
                  pl.BlockSpec((1, d), lambda i: (0, 0))],
        out_specs=pl.BlockSpec((tm, d), lambda i: (i, 0)),
        out_shape=jax.ShapeDtypeStruct((m, d), out_dtype),
        compiler_params=_params(("parallel",)),
        name="rmsnorm",
    )(x2d, g.reshape(1, d))


def _rope(xc, cos, sin):
    return xc * cos + pltpu.roll(xc, HEAD_DIM // 2, 1) * sin


def _proj_kernel(a_ref, w_ref, cos_ref, sin_ref, o_ref, *, n_full, n_half, n_tiles, q_scales):
    j = pl.program_id(1)
    acc = _dot(a_ref[...], w_ref[0].astype(BF16))
    factor = jnp.float32(1.0)
    for lo, hi, f in q_scales:
        factor = jnp.where((j >= lo) & (j <= hi), jnp.float32(f), factor)

    def store(n_roped):
        for c in range(PROJ_TN // LANES):
            sl = slice(c * LANES, (c + 1) * LANES)
            xc = acc[:, sl]
            if c < n_roped:
                xc = _rope(xc, cos_ref[...], sin_ref[...])
            o_ref[:, sl] = (xc * factor).astype(o_ref.dtype)

    pl.when(j < n_full)(lambda: store(PROJ_TN // LANES))
    if n_half:
        pl.when((j >= n_full) & (j < n_full + n_half))(lambda: store(NSA_KV_HEADS))
    if n_full + n_half < n_tiles:
        pl.when(j >= n_full + n_half)(lambda: store(0))


def _proj(h, w3d, layer, src_tile, n_tiles, n_full, n_half, q_scales, cosf, sinf, seq):
    m, k = h.shape
    tm = 1024
    t_blocks = seq // tm
    return pl.pallas_call(
        functools.partial(_proj_kernel, n_full=n_full, n_half=n_half, n_tiles=n_tiles, q_scales=q_scales),
        grid=(m // tm, n_tiles),
        in_specs=[pl.BlockSpec((tm, k), lambda i, j: (i, 0)),
                  pl.BlockSpec((1, k, PROJ_TN), lambda i, j: (layer, 0, src_tile(j))),
                  pl.BlockSpec((tm, LANES), lambda i, j: (i % t_blocks, 0)),
                  pl.BlockSpec((tm, LANES), lambda i, j: (i % t_blocks, 0))],
        out_specs=pl.BlockSpec((tm, PROJ_TN), lambda i, j: (i, j)),
        out_shape=jax.ShapeDtypeStruct((m, n_tiles * PROJ_TN), BF16),
        compiler_params=_params(("parallel", "arbitrary")),
        name="proj",
    )(h, w3d, cosf, sinf)


def _proj_b_kernel(a_ref, w_ref, wg_ref, cos_ref, sin_ref, o_ref):
    a = a_ref[...]
    acc = _dot(a, w_ref[0].astype(BF16))
    for c in range(B_GATE):
        xc = acc[:, c * LANES:(c + 1) * LANES]
        o_ref[c] = _rope(xc, cos_ref[...], sin_ref[...]) if c < NSA_KV_HEADS else xc
    o_ref[B_GATE] = _dot(a, wg_ref[0].astype(BF16))


def _proj_b(h, w3d, layer, cosf, sinf, seq):
    m, k = h.shape
    tm = 1024
    t_blocks = seq // tm
    return pl.pallas_call(
        _proj_b_kernel,
        grid=(m // tm,),
        in_specs=[pl.BlockSpec((tm, k), lambda i: (i, 0)),
                  pl.BlockSpec((1, k, 2 * KV_W), lambda i: (layer, 0, W_KCMP // (2 * KV_W))),
                  pl.BlockSpec((1, k, LANES), lambda i: (layer, 0, W_GATE // LANES)),
                  pl.BlockSpec((tm, LANES), lambda i: (i % t_blocks, 0)),
                  pl.BlockSpec((tm, LANES), lambda i: (i % t_blocks, 0))],
        out_specs=pl.BlockSpec((B_CHUNKS, tm, LANES), lambda i: (0, i, 0)),
        out_shape=jax.ShapeDtypeStruct((B_CHUNKS, m, LANES), F32),
        compiler_params=_params(("parallel",)),
        name="proj_b",
    )(h, w3d, w3d, cosf, sinf)


def _compress_kernel(x_ref, pe_ref, w1_ref, w2_ref, o_ref):
    blk = (x_ref[...] + pe_ref[0]).astype(BF16)
    hid = _dot(blk, w1_ref[0])
    hid = hid * jax.nn.sigmoid(hid)
    o_ref[0] = _dot(hid.astype(BF16), w2_ref[0])


def _compress(x_kv, pe, w1, w2):
    rows = x_kv.shape[0] // 2
    kdim = x_kv.shape[1]
    hidden = w1.shape[-1]
    return pl.pallas_call(
        _compress_kernel,
        grid=(2,),
        in_specs=[pl.BlockSpec((rows, kdim), lambda i: (i, 0)),
                  pl.BlockSpec((1, 1, kdim), lambda i: (i, 0, 0)),
                  pl.BlockSpec((1, kdim, hidden), lambda i: (i, 0, 0)),
                  pl.BlockSpec((1, hidden, HEAD_DIM), lambda i: (i, 0, 0))],
        out_specs=pl.BlockSpec((1, rows, HEAD_DIM), lambda i: (i, 0, 0)),
        out_shape=jax.ShapeDtypeStruct((2, rows, HEAD_DIM), F32),
        compiler_params=_params(("parallel",)),
        name="nsa_compress",
    )(x_kv, pe, w1, w2)


NSA_QB = 128
NSA_KC = 128
NSA_CW = 512
NSA_WIN_KEYS = NSA_WINDOW + NSA_QB
NB_PAD = 128


def _nsa_kernel(q_ref, ks_ref, kw_ref, vs_ref, vw_ref, kc_ref, vct_ref, gate_ref, o_ref,
                vst_ref, vwt_ref, val_ref, sel_ref, gt_ref, acc_ref, base_ref, *, seq):
    i = pl.program_id(2)
    qb = NSA_QB
    n_blocks = seq // NSA_BLOCK
    q0 = pl.multiple_of(i * qb, qb)

    @pl.when(i == 0)
    def _():
        for c in range(seq // NSA_KC):
            rows = slice(c * NSA_KC, (c + 1) * NSA_KC)
            vst_ref[c] = vs_ref[rows, :].astype(F32).T.astype(BF16)
            vwt_ref[c] = vw_ref[rows, :].astype(F32).T.astype(BF16)

    qs = [q_ref[:, r * LANES:(r + 1) * LANES] for r in range(NSA_GROUP)]
    t_row = i * qb + lax.broadcasted_iota(jnp.int32, (1, qb), 1)
    sub_n = lax.broadcasted_iota(jnp.int32, (NB_PAD, qb), 0)
    t_full = i * qb + lax.broadcasted_iota(jnp.int32, (NB_PAD, qb), 1)

    cmp_ok = (sub_n * NSA_BLOCK + (NSA_BLOCK - 1) <= t_full) & (sub_n < n_blocks)
    any_ok = jnp.where(t_row >= NSA_BLOCK - 1, 1.0, 0.0)
    kc = kc_ref[0]
    vct = vct_ref[0]
    imp = jnp.zeros((NB_PAD, qb), F32)
    o_cmp = []
    for r in range(NSA_GROUP):
        s = jnp.where(cmp_ok, _nt_dot(kc, qs[r]), NEG)
        e = jnp.exp2(s - jnp.max(s, axis=0, keepdims=True))
        p = e / jnp.sum(e, axis=0, keepdims=True) * any_ok
        imp = imp + p
        o_cmp.append(_dot(vct, p.astype(BF16)))

    cur = jnp.right_shift(t_full, 6)
    started = sub_n <= cur
    forced = (sub_n == 0) | (sub_n == cur) | (sub_n == cur - 1)
    val = jnp.where(forced, NSA_FORCED_SCORE, jnp.where(started, imp, -1.0))
    val_ref[...] = val
    val_n = val[:n_blocks]
    sub_nn = sub_n[:n_blocks]
    rank = jnp.zeros((n_blocks, qb), F32)
    for n2 in range(n_blocks):
        row = val_ref[n2:n2 + 1, :]
        rank = rank + jnp.where(sub_nn > n2, jnp.where(row >= val_n, 1.0, 0.0),
                                jnp.where(row > val_n, 1.0, 0.0))
    sel_ref[0:n_blocks, :] = jnp.where(rank < NSA_TOPN, 1.0, 0.0)

    per_step = NSA_CW // NSA_BLOCK
    d_cw = (lax.broadcasted_iota(jnp.int32, (NSA_CW, qb), 0)
            - lax.broadcasted_iota(jnp.int32, (NSA_CW, qb), 1))

    def slc_scores(cw):
        k0 = cw * NSA_CW if isinstance(cw, int) else pl.multiple_of(cw * NSA_CW, NSA_CW)
        k_chunk = ks_ref[pl.ds(k0, NSA_CW), :]
        picked = jnp.concatenate(
            [jnp.broadcast_to(sel_ref[pl.ds(per_step * cw + b, 1), :], (NSA_BLOCK, qb))
             for b in range(per_step)], axis=0)
        mask = jnp.where(d_cw <= q0 - k0, picked, 0.0) > 0.5
        return [jnp.where(mask, _nt_dot(k_chunk, qs[r]), NEG) for r in range(NSA_GROUP)]

    def pv(vt_ref, first_chunk, p, n_chunks):
        out = None
        for j in range(n_chunks):
            part = _dot(vt_ref[first_chunk + j], p[j * NSA_KC:(j + 1) * NSA_KC].astype(BF16))
            out = part if out is None else out + part
        return out

    ms, ls = [], []
    for r, s in enumerate(slc_scores(0)):
        m = jnp.max(s, axis=0, keepdims=True)
        p = jnp.exp2(s - m)
        ms.append(m)
        ls.append(jnp.sum(p, axis=0, keepdims=True))
        acc_ref[r] = pv(vst_ref, 0, p, NSA_CW // NSA_KC)

    w_start = pl.multiple_of(jnp.maximum(q0 - NSA_WINDOW, 0), NSA_KC)
    delta = ((q0 - w_start) + lax.broadcasted_iota(jnp.int32, (NSA_WIN_KEYS, qb), 1)
             - lax.broadcasted_iota(jnp.int32, (NSA_WIN_KEYS, qb), 0))
    in_win = (delta >= 0) & (delta < NSA_WINDOW)
    k_win = kw_ref[pl.ds(w_start, NSA_WIN_KEYS), :]
    first_chunk = jnp.right_shift(w_start, NSA_KC.bit_length() - 1)
    gt_ref[...] = jax.nn.sigmoid(gate_ref[0].T)
    g_row = pl.program_id(1) * (NSA_GROUP * 3)
    for r in range(NSA_GROUP):
        s = jnp.where(in_win, _nt_dot(k_win, qs[r]), NEG)
        p = jnp.exp2(s - jnp.max(s, axis=0, keepdims=True))
        l_win = jnp.sum(p, axis=0, keepdims=True)
        o_win = pv(vwt_ref, first_chunk, p, NSA_WIN_KEYS // NSA_KC)
        g_cmp = gt_ref[pl.ds(g_row + 3 * r, 1), :]
        g_win = gt_ref[pl.ds(g_row + 3 * r + 2, 1), :]
        base_ref[r] = g_cmp * o_cmp[r] + (g_win / l_win) * o_win

    def slc_body(cw, carry):
        ms, ls = carry
        new_m, new_l = [], []
        for r, s in enumerate(slc_scores(cw)):
            m_new = jnp.maximum(ms[r], jnp.max(s, axis=0, keepdims=True))
            alpha = jnp.exp2(ms[r] - m_new)
            p = jnp.exp2(s - m_new)
            new_m.append(m_new)
            new_l.append(alpha * ls[r] + jnp.sum(p, axis=0, keepdims=True))
            acc_ref[r] = alpha * acc_ref[r] + pv(vst_ref, cw * (NSA_CW // NSA_KC), p, NSA_CW // NSA_KC)
        return tuple(new_m), tuple(new_l)

    n_steps = jnp.right_shift(q0 + (qb + NSA_CW - 1), NSA_CW.bit_length() - 1)
    _, ls = lax.fori_loop(1, n_steps, slc_body, (tuple(ms), tuple(ls)))

    for r in range(NSA_GROUP):
        g_slc = gt_ref[pl.ds(g_row + 3 * r + 1, 1), :]
        o = base_ref[r] + (g_slc / ls[r]) * acc_ref[r]
        o_ref[:, r * LANES:(r + 1) * LANES] = o.T.astype(o_ref.dtype)


def _nsa_attention(proj_n, kc, vct, proj_b, batch, seq):
    qb = NSA_QB
    nq = seq // qb
    n_chunks = seq // NSA_KC
    gw = NSA_GROUP * LANES
    kv_spec = lambda base: pl.BlockSpec((seq, LANES), lambda b, g, i: (b, base + g))
    return pl.pallas_call(
        functools.partial(_nsa_kernel, seq=seq),
        grid=(batch, NSA_KV_HEADS, nq),
        in_specs=[
            pl.BlockSpec((qb, gw), lambda b, g, i: (b * nq + i, g)),
            kv_spec(CB_KSLC), kv_spec(CB_KWIN), kv_spec(CB_VSLC), kv_spec(CB_VWIN),
            pl.BlockSpec((1, NB_PAD, LANES), lambda b, g, i: (g * batch + b, 0, 0)),
            pl.BlockSpec((1, LANES, NB_PAD), lambda b, g, i: (g * batch + b, 0, 0)),
            pl.BlockSpec((1, qb, LANES), lambda b, g, i: (B_GATE, b * nq + i, 0)),
        ],
        out_specs=pl.BlockSpec((qb, gw), lambda b, g, i: (b * nq + i, g)),
        out_shape=jax.ShapeDtypeStruct((batch * seq, NSA_HEADS * LANES), BF16),
        scratch_shapes=[
            pltpu.VMEM((n_chunks, LANES, NSA_KC), BF16),
            pltpu.VMEM((n_chunks, LANES, NSA_KC), BF16),
            pltpu.VMEM((NB_PAD, qb), F32),
            pltpu.VMEM((NB_PAD, qb), F32),
            pltpu.VMEM((LANES, qb), F32),
            pltpu.VMEM((NSA_GROUP, LANES, qb), F32),
            pltpu.VMEM((NSA_GROUP, LANES, qb), F32),
        ],
        compiler_params=_params(("arbitrary", "arbitrary", "arbitrary")),
        name="nsa_attention",
    )(proj_n, proj_n, proj_n, proj_n, proj_n, kc, vct, proj_b)


DIL_QB = 256


def _dil_bias(window, dil):
    rows = jnp.arange(DIL_QB, dtype=jnp.int32)[:, None]
    cols = jnp.arange(window + DIL_QB, dtype=jnp.int32)[None, :]
    delta = window + rows - cols
    ok = (delta >= 0) & (delta <= window) & (delta % dil == 0)
    return jnp.where(ok, 0.0, NEG).astype(F32)


def _dil_kernel(q0_ref, q1_ref, q2_ref, k0_ref, k1_ref, k2_ref, v0_ref, v1_ref, v2_ref,
                b0_ref, b1_ref, b2_ref, o0_ref, o1_ref, o2_ref,
                kp0_ref, kp1_ref, kp2_ref, vp0_ref, vp1_ref, vp2_ref, *, seq):
    i = pl.program_id(2)
    qs = (q0_ref, q1_ref, q2_ref)
    ks = (k0_ref, k1_ref, k2_ref)
    vs = (v0_ref, v1_ref, v2_ref)
    biases = (b0_ref, b1_ref, b2_ref)
    kps = (kp0_ref, kp1_ref, kp2_ref)
    vps = (vp0_ref, vp1_ref, vp2_ref)
    outs = (o0_ref, o1_ref, o2_ref)

    @pl.when(i == 0)
    def _():
        for p, (window, _) in enumerate(DIL_PAIRS):
            kps[p][0:window, :] = jnp.zeros((window, LANES), BF16)
            vps[p][0:window, :] = jnp.zeros((window, LANES), BF16)
            kps[p][window:window + seq, :] = ks[p][...]
            vps[p][window:window + seq, :] = vs[p][...]

    q_start = pl.multiple_of(i * DIL_QB, DIL_QB)
    accs, lses = [], []
    for p, (window, _) in enumerate(DIL_PAIRS):
        width = window + DIL_QB
        k_win = kps[p][pl.ds(q_start, width), :]
        v_win = vps[p][pl.ds(q_start, width), :]
        col = lax.broadcasted_iota(jnp.int32, (1, width), 1)
        in_seq = jnp.where(col >= window - q_start, 0.0, NEG)
        s = _nt_dot(qs[p][...], k_win) + biases[p][...] + in_seq
        m = jnp.max(s, axis=1, keepdims=True)
        e = jnp.exp2(s - m)
        l = jnp.sum(e, axis=1, keepdims=True)
        accs.append(_dot(e.astype(BF16), v_win) / l)
        lses.append(m + jnp.log2(l))

    top = jnp.maximum(jnp.maximum(lses[0], lses[1]), lses[2])
    ws = [jnp.exp2(x - top) for x in lses]
    inv = 1.0 / (ws[0] + ws[1] + ws[2])
    for p in range(len(DIL_PAIRS)):
        outs[p][...] = (accs[p] * (ws[p] * inv)).astype(outs[p].dtype)


def _dil_attention(proj_d, batch, seq):
    nq = seq // DIL_QB
    hp = DIL_HEADS_PER_PAIR

    def q_spec(p):
        return pl.BlockSpec((DIL_QB, LANES), lambda b, h, i: (b * nq + i, CB_DQ + p * hp + h))

    def kv_spec(base, p):
        return pl.BlockSpec((seq, LANES), lambda b, h, i: (b, base + p * hp + h))

    def bias_spec(p):
        return pl.BlockSpec((DIL_QB, DIL_PAIRS[p][0] + DIL_QB), lambda b, h, i: (0, 0))

    out_spec = pl.BlockSpec((DIL_QB, LANES), lambda b, h, i: (b * nq + i, h))
    out_shape = jax.ShapeDtypeStruct((batch * seq, hp * LANES), BF16)
    staged = [pltpu.VMEM((w + seq, LANES), BF16) for w, _ in DIL_PAIRS]
    return pl.pallas_call(
        functools.partial(_dil_kernel, seq=seq),
        grid=(batch, hp, nq),
        in_specs=[q_spec(0), q_spec(1), q_spec(2),
                  kv_spec(CB_DK, 0), kv_spec(CB_DK, 1), kv_spec(CB_DK, 2),
                  kv_spec(CB_DV, 0), kv_spec(CB_DV, 1), kv_spec(CB_DV, 2),
                  bias_spec(0), bias_spec(1), bias_spec(2)],
        out_specs=[out_spec, out_spec, out_spec],
        out_shape=[out_shape, out_shape, out_shape],
        scratch_shapes=staged + staged,
        compiler_params=_params(("arbitrary", "arbitrary", "arbitrary")),
        name="dilated_attention",
    )(*([proj_d] * 9), *[_dil_bias(w, d) for w, d in DIL_PAIRS])


SB_QB = 256
SB_PAST = 256
SB_KB = 128
SB_DONE = -104.0


def _sb_suffix(n):
    j = jnp.arange(n, dtype=jnp.int32)[:, None]
    s = jnp.arange(n, dtype=jnp.int32)[None, :]
    return jnp.where(j > s, 1.0, 0.0).astype(BF16)


def _sb_kernel(q_ref, k_ref, v_ref, suffix_ref, o_ref):
    i = pl.program_id(2)
    width = SB_PAST + SB_QB
    q = q_ref[...]

    def tile(k_chunk, v_chunk, suffix, carry_sum, before):
        z = _nt_dot(q, k_chunk)
        soft = jnp.log1p(jnp.exp(-jnp.abs(z)))
        log_beta = jnp.minimum(z, 0.0) - soft
        log_1m = -jnp.maximum(z, 0.0) - soft
        if before is not None:
            log_1m = jnp.where(before, log_1m, 0.0)
        hi = log_1m.astype(BF16)
        lo = (log_1m - hi.astype(F32)).astype(BF16)
        logit = log_beta + (_dot(hi, suffix) + _dot(lo, suffix)) + carry_sum
        if before is not None:
            logit = jnp.where(before, logit, NEG)
        contrib = _dot(jnp.exp(logit).astype(BF16), v_chunk)
        return contrib, carry_sum + jnp.sum(log_1m, axis=1, keepdims=True)

    q_start = i * SB_QB
    w_start = pl.multiple_of(jnp.maximum(q_start - SB_PAST, 0), SB_QB)
    offs = (lax.broadcasted_iota(jnp.int32, (SB_QB, width), 1)
            - lax.broadcasted_iota(jnp.int32, (SB_QB, width), 0))
    before = offs < q_start - w_start
    acc, csum = tile(k_ref[pl.ds(w_start, width), :], v_ref[pl.ds(w_start, width), :],
                     suffix_ref[...], jnp.zeros((SB_QB, 1), F32), before)

    n_rest = jnp.right_shift(w_start, SB_KB.bit_length() - 1)
    suffix_kb = suffix_ref[0:SB_KB, 0:SB_KB]

    def cond(carry):
        c, _, _, live = carry
        return (c < n_rest) & (live > SB_DONE)

    def body(carry):
        c, acc, csum, _ = carry
        k_start = pl.multiple_of((n_rest - 1 - c) * SB_KB, SB_KB)
        contrib, csum = tile(k_ref[pl.ds(k_start, SB_KB), :], v_ref[pl.ds(k_start, SB_KB), :],
                             suffix_kb, csum, None)
        return c + 1, acc + contrib, csum, jnp.max(csum)

    _, acc, _, _ = lax.while_loop(cond, body, (jnp.int32(0), acc, csum, jnp.max(csum)))
    o_ref[...] = acc.astype(o_ref.dtype)


def _sb_attention(proj_d, batch, seq):
    nq = seq // SB_QB
    width = SB_PAST + SB_QB
    return pl.pallas_call(
        _sb_kernel,
        grid=(batch, SB_HEADS, nq),
        in_specs=[pl.BlockSpec((SB_QB, LANES), lambda b, h, i: (b * nq + i, CB_SQ + h)),
                  pl.BlockSpec((seq, LANES), lambda b, h, i: (b, CB_SK + h)),
                  pl.BlockSpec((seq, LANES), lambda b, h, i: (b, CB_SV + h)),
                  pl.BlockSpec((width, width), lambda b, h, i: (0, 0))],
        out_specs=pl.BlockSpec((SB_QB, LANES), lambda b, h, i: (b * nq + i, h)),
        out_shape=jax.ShapeDtypeStruct((batch * seq, SB_HEADS * LANES), BF16),
        compiler_params=_params(("parallel", "parallel", "arbitrary")),
        name="stick_breaking_attention",
    )(proj_d, proj_d, proj_d, _sb_suffix(width))


OUT_PIECES = (NSA_HEADS * LANES,) + (DIL_HEADS_PER_PAIR * LANES,) * 3 + (SB_HEADS * LANES,)


def _out_proj_kernel(a0, a1, a2, a3, a4, w_ref, x_ref, o_ref):
    acc = x_ref[...]
    off = 0
    for a_ref, width in zip((a0, a1, a2, a3, a4), OUT_PIECES):
        acc = acc + _dot(a_ref[...], w_ref[0, off:off + width, :].astype(BF16))
        off += width
    o_ref[...] = acc


def _out_proj(pieces, w3d, layer, x2d):
    m, n = x2d.shape
    tm, tn = 1024, 512
    a_specs = [pl.BlockSpec((tm, width), lambda i, j: (i, 0)) for width in OUT_PIECES]
    return pl.pallas_call(
        _out_proj_kernel,
        grid=(m // tm, n // tn),
        in_specs=a_specs + [pl.BlockSpec((1, w3d.shape[1], tn), lambda i, j: (layer, 0, j)),
                            pl.BlockSpec((tm, tn), lambda i, j: (i, j))],
        out_specs=pl.BlockSpec((tm, tn), lambda i, j: (i, j)),
        out_shape=jax.ShapeDtypeStruct((m, n), F32),
        compiler_params=_params(("parallel", "arbitrary")),
        name="out_proj",
    )(*pieces, w3d, x2d)


def _mlp_up_kernel(a_ref, w_ref, o_ref):
    r = jnp.maximum(_dot(a_ref[...], w_ref[0].astype(BF16)), 0.0)
    o_ref[...] = (r * r).astype(o_ref.dtype)


def _mlp_up(h, w3d, layer):
    m, k = h.shape
    n = w3d.shape[2]
    tm, tn = 2048, 256
    return pl.pallas_call(
        _mlp_up_kernel,
        grid=(m // tm, n // tn),
        in_specs=[pl.BlockSpec((tm, k), lambda i, j: (i, 0)),
                  pl.BlockSpec((1, k, tn), lambda i, j: (layer, 0, j))],
        out_specs=pl.BlockSpec((tm, tn), lambda i, j: (i, j)),
        out_shape=jax.ShapeDtypeStruct((m, n), BF16),
        compiler_params=_params(("parallel", "arbitrary")),
        name="mlp_up",
    )(h, w3d)


def _mlp_down_kernel(a_ref, w_ref, x_ref, o_ref, acc_ref, *, nk):
    k = pl.program_id(2)
    part = _dot(a_ref[...], w_ref[0])

    @pl.when(k == 0)
    def _():
        acc_ref[...] = x_ref[...] + part

    @pl.when(k > 0)
    def _():
        acc_ref[...] += part

    @pl.when(k == nk - 1)
    def _():
        o_ref[...] = acc_ref[...]


def _mlp_down(a, w3d, layer, x2d):
    m, kdim = a.shape
    n = w3d.shape[2]
    tm, tn, tk = 1024, 1024, 2048
    nk = kdim // tk
    return pl.pallas_call(
        functools.partial(_mlp_down_kernel, nk=nk),
        grid=(m // tm, n // tn, nk),
        in_specs=[pl.BlockSpec((tm, tk), lambda i, j, k: (i, k)),
                  pl.BlockSpec((1, tk, tn), lambda i, j, k: (layer, k, j)),
                  pl.BlockSpec((tm, tn), lambda i, j, k: (i, j))],
        out_specs=pl.BlockSpec((tm, tn), lambda i, j, k: (i, j)),
        out_shape=jax.ShapeDtypeStruct((m, n), F32),
        scratch_shapes=[pltpu.VMEM((tm, tn), F32)],
        compiler_params=_params(("parallel", "parallel", "arbitrary")),
        name="mlp_down",
    )(a, w3d, x2d)


def _rope_tables(seq):
    inv = 1.0 / (ROPE_THETA ** (jnp.arange(0, HEAD_DIM, 2, dtype=F32) / HEAD_DIM))
    ang = jnp.arange(seq, dtype=F32)[:, None] * inv[None, :]
    cos, sin = jnp.cos(ang), jnp.sin(ang)
    return jnp.concatenate([cos, cos], axis=-1), jnp.concatenate([-sin, sin], axis=-1)


def _nsa_mixer(proj_n, proj_b, batch, seq, pe_k, pe_v, w_ck1, w_ck2, w_cv1, w_cv2):
    nb = seq // NSA_BLOCK
    groups = NSA_KV_HEADS
    x_kv = proj_b[:B_GATE].reshape(2 * groups * batch * nb, NSA_BLOCK * HEAD_DIM)
    pe = jnp.stack([pe_k.reshape(1, -1), pe_v.reshape(1, -1)])
    w1 = jnp.stack([w_ck1, w_cv1]).astype(BF16)
    w2 = jnp.stack([w_ck2, w_cv2]).astype(BF16)
    cmp_kv = _compress(x_kv, pe, w1, w2)
    cmp_kv = cmp_kv.reshape(2, groups * batch, nb, HEAD_DIM)
    cmp_kv = jnp.pad(cmp_kv, ((0, 0), (0, 0), (0, NB_PAD - nb), (0, 0))).astype(BF16)
    kc = cmp_kv[0]
    vct = cmp_kv[1].transpose(0, 2, 1)
    return _nsa_attention(proj_n, kc, vct, proj_b, batch, seq)


def _layer(x2d, batch, seq, cosf, sinf, layer, w_in, w_tail, g_mix, w_out, g_mlp, w_up, w_down,
           pe_k, pe_v, w_ck1, w_ck2, w_cv1, w_cv2):
    h = _rmsnorm(x2d, g_mix, BF16)
    proj_n = _proj(h, w_in, layer, lambda j: jnp.where(j >= 3, j + 1, j), N_COLS // PROJ_TN, 3, 2,
                   ((0, 2, SCALE_LOG2E),), cosf, sinf, seq)
    q_tiles = DIL_HEADS * LANES // PROJ_TN
    sq_tile = CB_SQ * LANES // PROJ_TN
    proj_d = _proj(h, w_tail, layer, lambda j: j, A_COLS // PROJ_TN, 2 * q_tiles, 0,
                   ((0, q_tiles - 1, SCALE_LOG2E), (sq_tile, sq_tile + SB_HEADS * LANES // PROJ_TN - 1, SCALE)),
                   cosf, sinf, seq)
    proj_b = _proj_b(h, w_in, layer, cosf, sinf, seq)

    o_a = _nsa_mixer(proj_n, proj_b, batch, seq, pe_k, pe_v, w_ck1, w_ck2, w_cv1, w_cv2)
    o_b = _dil_attention(proj_d, batch, seq)
    o_c = _sb_attention(proj_d, batch, seq)

    x2d = _out_proj([o_a, *o_b, o_c], w_out, layer, x2d)
    h2 = _rmsnorm(x2d, g_mlp, BF16)
    up = _mlp_up(h2, w_up, layer)
    return _mlp_down(up, w_down, layer, x2d)


def kernel(x, w_in, w_out, norm_mix, norm_mlp, w_up, w_down, nsa_pe_k, nsa_pe_v,
           nsa_w_ck1, nsa_w_ck2, nsa_w_cv1, nsa_w_cv2, final_norm):
    batch, seq, d = x.shape
    cosf, sinf = _rope_tables(seq)
    depth = w_in.shape[0]
    w_tail = w_in[:, :, W_TAIL:].astype(BF16)
    w_head = w_in[:, :, :W_GATE + LANES]
    w_down = w_down.astype(BF16)
    x2d = x.reshape(batch * seq, d)
    for layer in range(depth):
        x2d = _layer(x2d, batch, seq, cosf, sinf, layer, w_head, w_tail, norm_mix[layer], w_out,
                     norm_mlp[layer], w_up, w_down, nsa_pe_k[layer], nsa_pe_v[layer],
                     nsa_w_ck1[layer], nsa_w_ck2[layer], nsa_w_cv1[layer], nsa_w_cv2[layer])
    return _rmsnorm(x2d, final_norm, x.dtype).reshape(batch, seq, d)
```

```python
import functools

import jax
import jax.numpy as jnp
from jax import lax
from jax.experimental import pallas as pl
from jax.experimental.pallas import tpu as pltpu

F32 = jnp.float32
BF16 = jnp.bfloat16

D_MODEL = 4096
HEAD_DIM = 128
ROPE_THETA = 10000.0
EPS = 1e-6
NEG = -1e30
NSA_HEADS = 12
NSA_KV_HEADS = 2
NSA_GROUP = NSA_HEADS // NSA_KV_HEADS
NSA_BLOCK = 64
NSA_TOPN = 16
NSA_WINDOW = 512
NSA_FORCED_SCORE = NSA_GROUP + 1.0
DIL_PAIRS = ((128, 1), (512, 4), (2048, 16))
DIL_HEADS_PER_PAIR = 4
DIL_HEADS = DIL_HEADS_PER_PAIR * len(DIL_PAIRS)
SB_HEADS = 8
D_FF = 4 * D_MODEL
SCALE = HEAD_DIM ** -0.5
SCALE_LOG2E = SCALE * 1.4426950408889634

LANES = 128
VMEM_LIMIT = 56 * 1024 * 1024

KV_W = NSA_KV_HEADS * HEAD_DIM
W_KCMP = NSA_HEADS * HEAD_DIM
W_GATE = W_KCMP + 6 * KV_W
N_GATE = NSA_HEADS * 3
W_TAIL = W_GATE + N_GATE
PROJ_TN = 512
N_COLS = 20 * LANES
CB_NSA_Q, CB_KSLC, CB_VSLC, CB_KWIN, CB_VWIN = 0, 12, 14, 16, 18
A_COLS = 60 * LANES
CB_DQ, CB_DK, CB_DV = 0, 12, 24
CB_SQ, CB_SK, CB_SV = 36, 44, 52
B_CHUNKS = 5
B_GATE = 4


def _params(sem):
    return pltpu.CompilerParams(dimension_semantics=sem, vmem_limit_bytes=VMEM_LIMIT)


def _nt_dot(a, b):
    return lax.dot_general(a, b, (((1,), (1,)), ((), ())), preferred_element_type=F32)


def _dot(a, b):
    return jnp.dot(a, b, preferred_element_type=F32)


def _rmsnorm_kernel(x_ref, g_ref, o_ref):
    x = x_ref[...]
    ms = jnp.mean(x * x, axis=-1, keepdims=True)
    o_ref[...] = (x * lax.rsqrt(ms + EPS) * g_ref[...]).astype(o_ref.dtype)


def _rmsnorm(x2d, g, out_dtype):
    m, d = x2d.shape
    tm = 256
    return pl.pallas_call(
        _rmsnorm_kernel,
        grid=(m // tm,),
        in_specs=[pl.BlockSpec((tm, d), lambda i: (i, 0)),
                  pl.BlockSpec((1, d), lambda i: (0, 0))],
        out_specs=pl.BlockSpec((tm, d), lambda i: (i, 0)),
        out_shape=jax.ShapeDtypeStruct((m, d), out_dtype),
        compiler_params=_params(("parallel",)),
        name="rmsnorm",
    )(x2d, g.reshape(1, d))


def _rope(xc, cos, sin):
    return xc * cos + pltpu.roll(xc, HEAD_DIM // 2, 1) * sin


def _proj_kernel(a_ref, w_ref, cos_ref, sin_ref, o_ref, *, n_full, n_half, n_tiles, q_scales):
    j = pl.program_id(1)
    acc = _dot(a_ref[...], w_ref[0].astype(BF16))
    factor = jnp.float32(1.0)
    for lo, hi, f in q_scales:
        factor = jnp.where((j >= lo) & (j <= hi), jnp.float32(f), factor)

    def store(n_roped):
        for c in range(PROJ_TN // LANES):
            sl = slice(c * LANES, (c + 1) * LANES)
            xc = acc[:, sl]
            if c < n_roped:
                xc = _rope(xc, cos_ref[...], sin_ref[...])
            o_ref[:, sl] = (xc * factor).astype(o_ref.dtype)

    pl.when(j < n_full)(lambda: store(PROJ_TN // LANES))
    if n_half:
        pl.when((j >= n_full) & (j < n_full + n_half))(lambda: store(NSA_KV_HEADS))
    if n_full + n_half < n_tiles:
        pl.when(j >= n_full + n_half)(lambda: store(0))


def _proj(h, w3d, layer, src_tile, n_tiles, n_full, n_half, q_scales, cosf, sinf, seq):
    m, k = h.shape
    tm = 1024
    t_blocks = seq // tm
    return pl.pallas_call(
        functools.partial(_proj_kernel, n_full=n_full, n_half=n_half, n_tiles=n_tiles, q_scales=q_scales),
        grid=(m // tm, n_tiles),
        in_specs=[pl.BlockSpec((tm, k), lambda i, j: (i, 0)),
                  pl.BlockSpec((1, k, PROJ_TN), lambda i, j: (layer, 0, src_tile(j))),
                  pl.BlockSpec((tm, LANES), lambda i, j: (i % t_blocks, 0)),
                  pl.BlockSpec((tm, LANES), lambda i, j: (i % t_blocks, 0))],
        out_specs=pl.BlockSpec((tm, PROJ_TN), lambda i, j: (i, j)),
        out_shape=jax.ShapeDtypeStruct((m, n_tiles * PROJ_TN), BF16),
        compiler_params=_params(("parallel", "arbitrary")),
        name="proj",
    )(h, w3d, cosf, sinf)


def _proj_b_kernel(a_ref, w_ref, wg_ref, cos_ref, sin_ref, o_ref):
    a = a_ref[...]
    acc = _dot(a, w_ref[0].astype(BF16))
    for c in range(B_GATE):
        xc = acc[:, c * LANES:(c + 1) * LANES]
        o_ref[c] = _rope(xc, cos_ref[...], sin_ref[...]) if c < NSA_KV_HEADS else xc
    o_ref[B_GATE] = _dot(a, wg_ref[0].astype(BF16))


def _proj_b(h, w3d, layer, cosf, sinf, seq):
    m, k = h.shape
    tm = 1024
    t_blocks = seq // tm
    return pl.pallas_call(
        _proj_b_kernel,
        grid=(m // tm,),
        in_specs=[pl.BlockSpec((tm, k), lambda i: (i, 0)),
                  pl.BlockSpec((1, k, 2 * KV_W), lambda i: (layer, 0, W_KCMP // (2 * KV_W))),
                  pl.BlockSpec((1, k, LANES), lambda i: (layer, 0, W_GATE // LANES)),
                  pl.BlockSpec((tm, LANES), lambda i: (i % t_blocks, 0)),
                  pl.BlockSpec((tm, LANES), lambda i: (i % t_blocks, 0))],
        out_specs=pl.BlockSpec((B_CHUNKS, tm, LANES), lambda i: (0, i, 0)),
        out_shape=jax.ShapeDtypeStruct((B_CHUNKS, m, LANES), F32),
        compiler_params=_params(("parallel",)),
        name="proj_b",
    )(h, w3d, w3d, cosf, sinf)


def _compress_kernel(x_ref, pe_ref, w1_ref, w2_ref, o_ref):
    blk = (x_ref[...] + pe_ref[0]).astype(BF16)
    hid = _dot(blk, w1_ref[0])
    hid = hid * jax.nn.sigmoid(hid)
    o_ref[0] = _dot(hid.astype(BF16), w2_ref[0])


def _compress(x_kv, pe, w1, w2):
    rows = x_kv.shape[0] // 2
    kdim = x_kv.shape[1]
    hidden = w1.shape[-1]
    return pl.pallas_call(
        _compress_kernel,
        grid=(2,),
        in_specs=[pl.BlockSpec((rows, kdim), lambda i: (i, 0)),
                  pl.BlockSpec((1, 1, kdim), lambda i: (i, 0, 0)),
                  pl.BlockSpec((1, kdim, hidden), lambda i: (i, 0, 0)),
                  pl.BlockSpec((1, hidden, HEAD_DIM), lambda i: (i, 0, 0))],
        out_specs=pl.BlockSpec((1, rows, HEAD_DIM), lambda i: (i, 0, 0)),
        out_shape=jax.ShapeDtypeStruct((2, rows, HEAD_DIM), F32),
        compiler_params=_params(("parallel",)),
        name="nsa_compress",
    )(x_kv, pe, w1, w2)


NSA_QB = 256
NSA_KC = 128
NSA_CW = 512
NSA_WIN_KEYS = NSA_WINDOW + NSA_QB
NB_PAD = 128


def _nsa_kernel(q_ref, ks_ref, kw_ref, vs_ref, vw_ref, kc_ref, vct_ref, gate_ref, o_ref,
                vst_ref, vwt_ref, val_ref, sel_ref, gt_ref, acc_ref, base_ref, *, seq):
    i = pl.program_id(2)
    qb = NSA_QB
    n_blocks = seq // NSA_BLOCK
    q0 = pl.multiple_of(i * qb, qb)

    @pl.when(i == 0)
    def _():
        for c in range(seq // NSA_KC):
            rows = slice(c * NSA_KC, (c + 1) * NSA_KC)
            vst_ref[c] = vs_ref[rows, :].astype(F32).T.astype(BF16)
            vwt_ref[c] = vw_ref[rows, :].astype(F32).T.astype(BF16)

    qs = [q_ref[:, r * LANES:(r + 1) * LANES] for r in range(NSA_GROUP)]
    t_row = i * qb + lax.broadcasted_iota(jnp.int32, (1, qb), 1)
    sub_n = lax.broadcasted_iota(jnp.int32, (NB_PAD, qb), 0)
    t_full = i * qb + lax.broadcasted_iota(jnp.int32, (NB_PAD, qb), 1)

    cmp_ok = (sub_n * NSA_BLOCK + (NSA_BLOCK - 1) <= t_full) & (sub_n < n_blocks)
    any_ok = jnp.where(t_row >= NSA_BLOCK - 1, 1.0, 0.0)
    kc = kc_ref[0]
    vct = vct_ref[0]
    imp = jnp.zeros((NB_PAD, qb), F32)
    o_cmp = []
    for r in range(NSA_GROUP):
        s = jnp.where(cmp_ok, _nt_dot(kc, qs[r]), NEG)
        e = jnp.exp2(s - jnp.max(s, axis=0, keepdims=True))
        p = e / jnp.sum(e, axis=0, keepdims=True) * any_ok
        imp = imp + p
        o_cmp.append(_dot(vct, p.astype(BF16)))

    cur = jnp.right_shift(t_full, 6)
    started = sub_n <= cur
    forced = (sub_n == 0) | (sub_n == cur) | (sub_n == cur - 1)
    val = jnp.where(forced, NSA_FORCED_SCORE, jnp.where(started, imp, -1.0))
    val_ref[...] = val
    val_n = val[:n_blocks]
    sub_nn = lax.broadcasted_iota(jnp.int32, (n_blocks, qb), 0)
    rank = jnp.zeros((n_blocks, qb), F32)
    for n2 in range(n_blocks):
        row = val_ref[n2:n2 + 1, :]
        rank = rank + jnp.where(sub_nn > n2, jnp.where(row >= val_n, 1.0, 0.0),
                                jnp.where(row > val_n, 1.0, 0.0))
    sel_ref[0:n_blocks, :] = jnp.where(rank < NSA_TOPN, 1.0, 0.0)

    per_step = NSA_CW // NSA_BLOCK
    d_cw = (lax.broadcasted_iota(jnp.int32, (NSA_CW, qb), 0)
            - lax.broadcasted_iota(jnp.int32, (NSA_CW, qb), 1))

    def slc_scores(cw):
        k0 = cw * NSA_CW if isinstance(cw, int) else pl.multiple_of(cw * NSA_CW, NSA_CW)
        k_chunk = ks_ref[pl.ds(k0, NSA_CW), :]
        picked = jnp.concatenate(
            [jnp.broadcast_to(sel_ref[pl.ds(per_step * cw + b, 1), :], (NSA_BLOCK, qb))
             for b in range(per_step)], axis=0)
        mask = jnp.where(d_cw <= q0 - k0, picked, 0.0) > 0.5
        return [jnp.where(mask, _nt_dot(k_chunk, qs[r]), NEG) for r in range(NSA_GROUP)]

    def pv(vt_ref, first_chunk, p, n_chunks):
        out = None
        for j in range(n_chunks):
            part = _dot(vt_ref[first_chunk + j], p[j * NSA_KC:(j + 1) * NSA_KC].astype(BF16))
            out = part if out is None else out + part
        return out

    ms, ls = [], []
    for r, s in enumerate(slc_scores(0)):
        m = jnp.max(s, axis=0, keepdims=True)
        p = jnp.exp2(s - m)
        ms.append(m)
        ls.append(jnp.sum(p, axis=0, keepdims=True))
        acc_ref[r] = pv(vst_ref, 0, p, NSA_CW // NSA_KC)

    w_start = pl.multiple_of(jnp.maximum(q0 - NSA_WINDOW, 0), NSA_KC)
    delta = ((q0 - w_start) + lax.broadcasted_iota(jnp.int32, (NSA_WIN_KEYS, qb), 1)
             - lax.broadcasted_iota(jnp.int32, (NSA_WIN_KEYS, qb), 0))
    in_win = (delta >= 0) & (delta < NSA_WINDOW)
    k_win = kw_ref[pl.ds(w_start, NSA_WIN_KEYS), :]
    first_chunk = jnp.right_shift(w_start, NSA_KC.bit_length() - 1)
    gt_ref[...] = jax.nn.sigmoid(gate_ref[0].T)
    g_row = pl.program_id(1) * (NSA_GROUP * 3)
    for r in range(NSA_GROUP):
        s = jnp.where(in_win, _nt_dot(k_win, qs[r]), NEG)
        p = jnp.exp2(s - jnp.max(s, axis=0, keepdims=True))
        l_win = jnp.sum(p, axis=0, keepdims=True)
        o_win = pv(vwt_ref, first_chunk, p, NSA_WIN_KEYS // NSA_KC)
        g_cmp = gt_ref[pl.ds(g_row + 3 * r, 1), :]
        g_win = gt_ref[pl.ds(g_row + 3 * r + 2, 1), :]
        base_ref[r] = g_cmp * o_cmp[r] + (g_win / l_win) * o_win

    def slc_body(cw, carry):
        ms, ls = carry
        new_m, new_l = [], []
        for r, s in enumerate(slc_scores(cw)):
            m_new = jnp.maximum(ms[r], jnp.max(s, axis=0, keepdims=True))
            alpha = jnp.exp2(ms[r] - m_new)
            p = jnp.exp2(s - m_new)
            new_m.append(m_new)
            new_l.append(alpha * ls[r] + jnp.sum(p, axis=0, keepdims=True))
            acc_ref[r] = alpha * acc_ref[r] + pv(vst_ref, cw * (NSA_CW // NSA_KC), p, NSA_CW // NSA_KC)
        return tuple(new_m), tuple(new_l)

    n_steps = jnp.right_shift(q0 + (qb + NSA_CW - 1), NSA_CW.bit_length() - 1)
    _, ls = lax.fori_loop(1, n_steps, slc_body, (tuple(ms), tuple(ls)))

    for r in range(NSA_GROUP):
        g_slc = gt_ref[pl.ds(g_row + 3 * r + 1, 1), :]
        o = base_ref[r] + (g_slc / ls[r]) * acc_ref[r]
        o_ref[:, r * LANES:(r + 1) * LANES] = o.T.astype(o_ref.dtype)


def _nsa_attention(proj_n, kc, vct, proj_b, batch, seq):
    qb = NSA_QB
    nq = seq // qb
    n_chunks = seq // NSA_KC
    gw = NSA_GROUP * LANES
    kv_spec = lambda base: pl.BlockSpec((seq, LANES), lambda b, g, i: (b, base + g))
    return pl.pallas_call(
        functools.partial(_nsa_kernel, seq=seq),
        grid=(batch, NSA_KV_HEADS, nq),
        in_specs=[
            pl.BlockSpec((qb, gw), lambda b, g, i: (b * nq + i, g)),
            kv_spec(CB_KSLC), kv_spec(CB_KWIN), kv_spec(CB_VSLC), kv_spec(CB_VWIN),
            pl.BlockSpec((1, NB_PAD, LANES), lambda b, g, i: (g * batch + b, 0, 0)),
            pl.BlockSpec((1, LANES, NB_PAD), lambda b, g, i: (g * batch + b, 0, 0)),
            pl.BlockSpec((1, qb, LANES), lambda b, g, i: (B_GATE, b * nq + i, 0)),
        ],
        out_specs=pl.BlockSpec((qb, gw), lambda b, g, i: (b * nq + i, g)),
        out_shape=jax.ShapeDtypeStruct((batch * seq, NSA_HEADS * LANES), BF16),
        scratch_shapes=[
            pltpu.VMEM((n_chunks, LANES, NSA_KC), BF16),
            pltpu.VMEM((n_chunks, LANES, NSA_KC), BF16),
            pltpu.VMEM((NB_PAD, qb), F32),
            pltpu.VMEM((NB_PAD, qb), F32),
            pltpu.VMEM((LANES, qb), F32),
            pltpu.VMEM((NSA_GROUP, LANES, qb), F32),
            pltpu.VMEM((NSA_GROUP, LANES, qb), F32),
        ],
        compiler_params=_params(("arbitrary", "arbitrary", "arbitrary")),
        name="nsa_attention",
    )(proj_n, proj_n, proj_n, proj_n, proj_n, kc, vct, proj_b)


DIL_QB = 256


def _dil_bias(window, dil):
    rows = jnp.arange(DIL_QB, dtype=jnp.int32)[:, None]
    cols = jnp.arange(window + DIL_QB, dtype=jnp.int32)[None, :]
    delta = window + rows - cols
    ok = (delta >= 0) & (delta <= window) & (delta % dil == 0)
    return jnp.where(ok, 0.0, NEG).astype(F32)


def _dil_kernel(q0_ref, q1_ref, q2_ref, k0_ref, k1_ref, k2_ref, v0_ref, v1_ref, v2_ref,
                b0_ref, b1_ref, b2_ref, o0_ref, o1_ref, o2_ref,
                kp0_ref, kp1_ref, kp2_ref, vp0_ref, vp1_ref, vp2_ref, *, seq):
    i = pl.program_id(2)
    qs = (q0_ref, q1_ref, q2_ref)
    ks = (k0_ref, k1_ref, k2_ref)
    vs = (v0_ref, v1_ref, v2_ref)
    biases = (b0_ref, b1_ref, b2_ref)
    kps = (kp0_ref, kp1_ref, kp2_ref)
    vps = (vp0_ref, vp1_ref, vp2_ref)
    outs = (o0_ref, o1_ref, o2_ref)

    @pl.when(i == 0)
    def _():
        for p, (window, _) in enumerate(DIL_PAIRS):
            kps[p][0:window, :] = jnp.zeros((window, LANES), BF16)
            vps[p][0:window, :] = jnp.zeros((window, LANES), BF16)
            kps[p][window:window + seq, :] = ks[p][...]
            vps[p][window:window + seq, :] = vs[p][...]

    q_start = pl.multiple_of(i * DIL_QB, DIL_QB)
    accs, lses = [], []
    for p, (window, _) in enumerate(DIL_PAIRS):
        width = window + DIL_QB
        k_win = kps[p][pl.ds(q_start, width), :]
        v_win = vps[p][pl.ds(q_start, width), :]
        col = lax.broadcasted_iota(jnp.int32, (1, width), 1)
        in_seq = jnp.where(col >= window - q_start, 0.0, NEG)
        s = _nt_dot(qs[p][...], k_win) + biases[p][...] + in_seq
        m = jnp.max(s, axis=1, keepdims=True)
        e = jnp.exp2(s - m)
        l = jnp.sum(e, axis=1, keepdims=True)
        accs.append(_dot(e.astype(BF16), v_win) / l)
        lses.append(m + jnp.log2(l))

    top = jnp.maximum(jnp.maximum(lses[0], lses[1]), lses[2])
    ws = [jnp.exp2(x - top) for x in lses]
    inv = 1.0 / (ws[0] + ws[1] + ws[2])
    for p in range(len(DIL_PAIRS)):
        outs[p][...] = (accs[p] * (ws[p] * inv)).astype(outs[p].dtype)


def _dil_attention(proj_d, batch, seq):
    nq = seq // DIL_QB
    hp = DIL_HEADS_PER_PAIR

    def q_spec(p):
        return pl.BlockSpec((DIL_QB, LANES), lambda b, h, i: (b * nq + i, CB_DQ + p * hp + h))

    def kv_spec(base, p):
        return pl.BlockSpec((seq, LANES), lambda b, h, i: (b, base + p * hp + h))

    def bias_spec(p):
        return pl.BlockSpec((DIL_QB, DIL_PAIRS[p][0] + DIL_QB), lambda b, h, i: (0, 0))

    out_spec = pl.BlockSpec((DIL_QB, LANES), lambda b, h, i: (b * nq + i, h))
    out_shape = jax.ShapeDtypeStruct((batch * seq, hp * LANES), BF16)
    staged = [pltpu.VMEM((w + seq, LANES), BF16) for w, _ in DIL_PAIRS]
    return pl.pallas_call(
        functools.partial(_dil_kernel, seq=seq),
        grid=(batch, hp, nq),
        in_specs=[q_spec(0), q_spec(1), q_spec(2),
                  kv_spec(CB_DK, 0), kv_spec(CB_DK, 1), kv_spec(CB_DK, 2),
                  kv_spec(CB_DV, 0), kv_spec(CB_DV, 1), kv_spec(CB_DV, 2),
                  bias_spec(0), bias_spec(1), bias_spec(2)],
        out_specs=[out_spec, out_spec, out_spec],
        out_shape=[out_shape, out_shape, out_shape],
        scratch_shapes=staged + staged,
        compiler_params=_params(("arbitrary", "arbitrary", "arbitrary")),
        name="dilated_attention",
    )(*([proj_d] * 9), *[_dil_bias(w, d) for w, d in DIL_PAIRS])


SB_QB = 256
SB_PAST = 256
SB_KB = 128
SB_DONE = -104.0


def _sb_suffix(n):
    j = jnp.arange(n, dtype=jnp.int32)[:, None]
    s = jnp.arange(n, dtype=jnp.int32)[None, :]
    return jnp.where(j > s, 1.0, 0.0).astype(BF16)


SB_HP = 2


def _sb_kernel(q_ref, k_ref, v_ref, suffix_ref, o_ref):
    i = pl.program_id(2)
    width = SB_PAST + SB_QB

    def tile(q, k_chunk, v_chunk, suffix, carry_sum, before):
        z = _nt_dot(q, k_chunk)
        soft = jnp.log(1.0 + jnp.exp(-jnp.abs(z)))
        log_beta = jnp.minimum(z, 0.0) - soft
        log_1m = -jnp.maximum(z, 0.0) - soft
        if before is not None:
            log_1m = jnp.where(before, log_1m, 0.0)
        hi = log_1m.astype(BF16)
        lo = (log_1m - hi.astype(F32)).astype(BF16)
        logit = log_beta + (_dot(hi, suffix) + _dot(lo, suffix)) + carry_sum
        if before is not None:
            logit = jnp.where(before, logit, NEG)
        contrib = _dot(jnp.exp(logit).astype(BF16), v_chunk)
        return contrib, carry_sum + jnp.sum(log_1m, axis=1, keepdims=True)

    q_start = i * SB_QB
    w_start = pl.multiple_of(jnp.maximum(q_start - SB_PAST, 0), SB_QB)
    offs = (lax.broadcasted_iota(jnp.int32, (SB_QB, width), 1)
            - lax.broadcasted_iota(jnp.int32, (SB_QB, width), 0))
    before = offs < q_start - w_start
    heads = [slice(h * LANES, (h + 1) * LANES) for h in range(SB_HP)]
    qs = [q_ref[:, sl] for sl in heads]
    state = [tile(qs[h], k_ref[pl.ds(w_start, width), heads[h]], v_ref[pl.ds(w_start, width), heads[h]],
                  suffix_ref[...], jnp.zeros((SB_QB, 1), F32), before) for h in range(SB_HP)]
    accs = tuple(s[0] for s in state)
    csums = tuple(s[1] for s in state)

    n_rest = jnp.right_shift(w_start, SB_KB.bit_length() - 1)
    suffix_kb = suffix_ref[0:SB_KB, 0:SB_KB]

    def live_of(csums):
        top = csums[0]
        for c in csums[1:]:
            top = jnp.maximum(top, c)
        return jnp.max(top)

    def cond(carry):
        c, _, _, live = carry
        return (c < n_rest) & (live > SB_DONE)

    def body(carry):
        c, accs, csums, _ = carry
        k_start = pl.multiple_of((n_rest - 1 - c) * SB_KB, SB_KB)
        new_a, new_c = [], []
        for h in range(SB_HP):
            contrib, cs = tile(qs[h], k_ref[pl.ds(k_start, SB_KB), heads[h]],
                               v_ref[pl.ds(k_start, SB_KB), heads[h]], suffix_kb, csums[h], None)
            new_a.append(accs[h] + contrib)
            new_c.append(cs)
        return c + 1, tuple(new_a), tuple(new_c), live_of(new_c)

    _, accs, _, _ = lax.while_loop(cond, body, (jnp.int32(0), accs, csums, live_of(csums)))
    for h in range(SB_HP):
        o_ref[:, heads[h]] = accs[h].astype(o_ref.dtype)


def _sb_attention(proj_d, batch, seq):
    nq = seq // SB_QB
    width = SB_PAST + SB_QB
    hw = SB_HP * LANES
    return pl.pallas_call(
        _sb_kernel,
        grid=(batch, SB_HEADS // SB_HP, nq),
        in_specs=[pl.BlockSpec((SB_QB, hw), lambda b, h, i: (b * nq + i, CB_SQ // SB_HP + h)),
                  pl.BlockSpec((seq, hw), lambda b, h, i: (b, CB_SK // SB_HP + h)),
                  pl.BlockSpec((seq, hw), lambda b, h, i: (b, CB_SV // SB_HP + h)),
                  pl.BlockSpec((width, width), lambda b, h, i: (0, 0))],
        out_specs=pl.BlockSpec((SB_QB, hw), lambda b, h, i: (b * nq + i, h)),
        out_shape=jax.ShapeDtypeStruct((batch * seq, SB_HEADS * LANES), BF16),
        compiler_params=_params(("parallel", "parallel", "arbitrary")),
        name="stick_breaking_attention",
    )(proj_d, proj_d, proj_d, _sb_suffix(width))


OUT_PIECES = (NSA_HEADS * LANES,) + (DIL_HEADS_PER_PAIR * LANES,) * 3 + (SB_HEADS * LANES,)


def _out_proj_kernel(a0, a1, a2, a3, a4, w_ref, x_ref, o_ref):
    acc = x_ref[...]
    off = 0
    for a_ref, width in zip((a0, a1, a2, a3, a4), OUT_PIECES):
        acc = acc + _dot(a_ref[...], w_ref[0, off:off + width, :].astype(BF16))
        off += width
    o_ref[...] = acc


def _out_proj(pieces, w3d, layer, x2d):
    m, n = x2d.shape
    tm, tn = 1024, 512
    a_specs = [pl.BlockSpec((tm, width), lambda i, j: (i, 0)) for width in OUT_PIECES]
    return pl.pallas_call(
        _out_proj_kernel,
        grid=(m // tm, n // tn),
        in_specs=a_specs + [pl.BlockSpec((1, w3d.shape[1], tn), lambda i, j: (layer, 0, j)),
                            pl.BlockSpec((tm, tn), lambda i, j: (i, j))],
        out_specs=pl.BlockSpec((tm, tn), lambda i, j: (i, j)),
        out_shape=jax.ShapeDtypeStruct((m, n), F32),
        compiler_params=_params(("parallel", "arbitrary")),
        name="out_proj",
    )(*pieces, w3d, x2d)


def _mlp_up_kernel(a_ref, w_ref, o_ref):
    r = jnp.maximum(_dot(a_ref[...], w_ref[0].astype(BF16)), 0.0)
    o_ref[...] = (r * r).astype(o_ref.dtype)


def _mlp_up(h, w3d, layer):
    m, k = h.shape
    n = w3d.shape[2]
    tm, tn = 2048, 256
    return pl.pallas_call(
        _mlp_up_kernel,
        grid=(m // tm, n // tn),
        in_specs=[pl.BlockSpec((tm, k), lambda i, j: (i, 0)),
                  pl.BlockSpec((1, k, tn), lambda i, j: (layer, 0, j))],
        out_specs=pl.BlockSpec((tm, tn), lambda i, j: (i, j)),
        out_shape=jax.ShapeDtypeStruct((m, n), BF16),
        compiler_params=_params(("parallel", "arbitrary")),
        name="mlp_up",
    )(h, w3d)


def _mlp_down_kernel(a_ref, w_ref, x_ref, o_ref, acc_ref, *, nk):
    k = pl.program_id(2)
    part = _dot(a_ref[...], w_ref[0])

    @pl.when(k == 0)
    def _():
        acc_ref[...] = x_ref[...] + part

    @pl.when(k > 0)
    def _():
        acc_ref[...] += part

    @pl.when(k == nk - 1)
    def _():
        o_ref[...] = acc_ref[...]


def _mlp_down(a, w3d, layer, x2d):
    m, kdim = a.shape
    n = w3d.shape[2]
    tm, tn, tk = 1024, 1024, 2048
    nk = kdim // tk
    return pl.pallas_call(
        functools.partial(_mlp_down_kernel, nk=nk),
        grid=(m // tm, n // tn, nk),
        in_specs=[pl.BlockSpec((tm, tk), lambda i, j, k: (i, k)),
                  pl.BlockSpec((1, tk, tn), lambda i, j, k: (layer, k, j)),
                  pl.BlockSpec((tm, tn), lambda i, j, k: (i, j))],
        out_specs=pl.BlockSpec((tm, tn), lambda i, j, k: (i, j)),
        out_shape=jax.ShapeDtypeStruct((m, n), F32),
        scratch_shapes=[pltpu.VMEM((tm, tn), F32)],
        compiler_params=_params(("parallel", "parallel", "arbitrary")),
        name="mlp_down",
    )(a, w3d, x2d)


def _rope_tables(seq):
    inv = 1.0 / (ROPE_THETA ** (jnp.arange(0, HEAD_DIM, 2, dtype=F32) / HEAD_DIM))
    ang = jnp.arange(seq, dtype=F32)[:, None] * inv[None, :]
    cos, sin = jnp.cos(ang), jnp.sin(ang)
    return jnp.concatenate([cos, cos], axis=-1), jnp.concatenate([-sin, sin], axis=-1)


def _nsa_mixer(proj_n, proj_b, batch, seq, pe_k, pe_v, w_ck1, w_ck2, w_cv1, w_cv2):
    nb = seq // NSA_BLOCK
    groups = NSA_KV_HEADS
    x_kv = proj_b[:B_GATE].reshape(2 * groups * batch * nb, NSA_BLOCK * HEAD_DIM)
    pe = jnp.stack([pe_k.reshape(1, -1), pe_v.reshape(1, -1)])
    w1 = jnp.stack([w_ck1, w_cv1]).astype(BF16)
    w2 = jnp.stack([w_ck2, w_cv2]).astype(BF16)
    cmp_kv = _compress(x_kv, pe, w1, w2)
    cmp_kv = cmp_kv.reshape(2, groups * batch, nb, HEAD_DIM)
    cmp_kv = jnp.pad(cmp_kv, ((0, 0), (0, 0), (0, NB_PAD - nb), (0, 0))).astype(BF16)
    kc = cmp_kv[0]
    vct = cmp_kv[1].transpose(0, 2, 1)
    return _nsa_attention(proj_n, kc, vct, proj_b, batch, seq)


def _layer(x2d, batch, seq, cosf, sinf, layer, w_in, w_tail, g_mix, w_out, g_mlp, w_up, w_down,
           pe_k, pe_v, w_ck1, w_ck2, w_cv1, w_cv2):
    h = _rmsnorm(x2d, g_mix, BF16)
    proj_n = _proj(h, w_in, layer, lambda j: jnp.where(j >= 3, j + 1, j), N_COLS // PROJ_TN, 3, 2,
                   ((0, 2, SCALE_LOG2E),), cosf, sinf, seq)
    q_tiles = DIL_HEADS * LANES // PROJ_TN
    sq_tile = CB_SQ * LANES // PROJ_TN
    proj_d = _proj(h, w_tail, layer, lambda j: j, A_COLS // PROJ_TN, 2 * q_tiles, 0,
                   ((0, q_tiles - 1, SCALE_LOG2E), (sq_tile, sq_tile + SB_HEADS * LANES // PROJ_TN - 1, SCALE)),
                   cosf, sinf, seq)
    proj_b = _proj_b(h, w_in, layer, cosf, sinf, seq)

    o_a = _nsa_mixer(proj_n, proj_b, batch, seq, pe_k, pe_v, w_ck1, w_ck2, w_cv1, w_cv2)
    o_b = _dil_attention(proj_d, batch, seq)
    o_c = _sb_attention(proj_d, batch, seq)

    x2d = _out_proj([o_a, *o_b, o_c], w_out, layer, x2d)
    h2 = _rmsnorm(x2d, g_mlp, BF16)
    up = _mlp_up(h2, w_up, layer)
    return _mlp_down(up, w_down, layer, x2d)


def kernel(x, w_in, w_out, norm_mix, norm_mlp, w_up, w_down, nsa_pe_k, nsa_pe_v,
           nsa_w_ck1, nsa_w_ck2, nsa_w_cv1, nsa_w_cv2, final_norm):
    batch, seq, d = x.shape
    cosf, sinf = _rope_tables(seq)
    depth = w_in.shape[0]
    w_head = w_in.astype(BF16)
    w_tail = w_head[:, :, W_TAIL:]
    w_down = w_down.astype(BF16)
    x2d = x.reshape(batch * seq, d)
    for layer in range(depth):
        x2d = _layer(x2d, batch, seq, cosf, sinf, layer, w_head, w_tail, norm_mix[layer], w_out,
                     norm_mlp[layer], w_up, w_down, nsa_pe_k[layer], nsa_pe_v[layer],
                     nsa_w_ck1[layer], nsa_w_ck2[layer], nsa_w_cv1[layer], nsa_w_cv2[layer])
    return _rmsnorm(x2d, final_norm, x.dtype).reshape(batch, seq, d)
```

```python
import functools

import jax
import jax.numpy as jnp
from jax import lax
from jax.experimental import pallas as pl
from jax.experimental.pallas import tpu as pltpu

F32 = jnp.float32
BF16 = jnp.bfloat16

D_MODEL = 4096
HEAD_DIM = 128
ROPE_THETA = 10000.0
EPS = 1e-6
NEG = -1e30
NSA_HEADS = 12
NSA_KV_HEADS = 2
NSA_GROUP = NSA_HEADS // NSA_KV_HEADS
NSA_BLOCK = 64
NSA_TOPN = 16
NSA_WINDOW = 512
NSA_FORCED_SCORE = NSA_GROUP + 1.0
DIL_PAIRS = ((128, 1), (512, 4), (2048, 16))
DIL_HEADS_PER_PAIR = 4
DIL_HEADS = DIL_HEADS_PER_PAIR * len(DIL_PAIRS)
SB_HEADS = 8
D_FF = 4 * D_MODEL
SCALE = HEAD_DIM ** -0.5
SCALE_LOG2E = SCALE * 1.4426950408889634

LANES = 128
VMEM_LIMIT = 56 * 1024 * 1024

KV_W = NSA_KV_HEADS * HEAD_DIM
W_KCMP = NSA_HEADS * HEAD_DIM
W_GATE = W_KCMP + 6 * KV_W
N_GATE = NSA_HEADS * 3
W_TAIL = W_GATE + N_GATE
PROJ_TN = 512
N_COLS = 20 * LANES
CB_NSA_Q, CB_KSLC, CB_VSLC, CB_KWIN, CB_VWIN = 0, 12, 14, 16, 18
A_COLS = 60 * LANES
CB_DQ, CB_DK, CB_DV = 0, 12, 24
CB_SQ, CB_SK, CB_SV = 36, 44, 52
B_CHUNKS = 5
B_GATE = 4


def _params(sem):
    return pltpu.CompilerParams(dimension_semantics=sem, vmem_limit_bytes=VMEM_LIMIT)


def _nt_dot(a, b):
    return lax.dot_general(a, b, (((1,), (1,)), ((), ())), preferred_element_type=F32)


def _dot(a, b):
    return jnp.dot(a, b, preferred_element_type=F32)


def _rmsnorm_kernel(x_ref, g_ref, o_ref):
    x = x_ref[...]
    ms = jnp.mean(x * x, axis=-1, keepdims=True)
    o_ref[...] = (x * lax.rsqrt(ms + EPS) * g_ref[...]).astype(o_ref.dtype)


def _rmsnorm(x2d, g, out_dtype):
    m, d = x2d.shape
    tm = 256
    return pl.pallas_call(
        _rmsnorm_kernel,
        grid=(m // tm,),
        in_specs=[pl.BlockSpec((tm, d), lambda i: (i, 0)),
                  pl.BlockSpec((1, d), lambda i: (0, 0))],
        out_specs=pl.BlockSpec((tm, d), lambda i: (i, 0)),
        out_shape=jax.ShapeDtypeStruct((m, d), out_dtype),
        compiler_params=_params(("parallel",)),
        name="rmsnorm",
    )(x2d, g.reshape(1, d))


def _rope(xc, cos, sin):
    return xc * cos + pltpu.roll(xc, HEAD_DIM // 2, 1) * sin


def _proj_kernel(a_ref, w_ref, cos_ref, sin_ref, o_ref, *, n_full, n_half, q_scales):
    j = pl.program_id(1)
    acc = _dot(a_ref[...], w_ref[0].astype(BF16))
    factor = jnp.float32(1.0)
    for lo, hi, f in q_scales:
        factor = jnp.where((j >= lo) & (j <= hi), jnp.float32(f), factor)

    n_roped = jnp.where(j < n_full, PROJ_TN // LANES, jnp.where(j < n_full + n_half, NSA_KV_HEADS, 0))
    cos_f = cos_ref[...] * factor
    sin_f = sin_ref[...] * factor
    for c in range(PROJ_TN // LANES):
        sl = slice(c * LANES, (c + 1) * LANES)
        roped = n_roped > c
        o_ref[:, sl] = _rope(acc[:, sl], jnp.where(roped, cos_f, factor),
                             jnp.where(roped, sin_f, 0.0)).astype(o_ref.dtype)


def _proj(h, w3d, layer, src_tile, n_tiles, n_full, n_half, q_scales, cosf, sinf, seq):
    m, k = h.shape
    tm = 1024
    t_blocks = seq // tm
    return pl.pallas_call(
        functools.partial(_proj_kernel, n_full=n_full, n_half=n_half, q_scales=q_scales),
        grid=(m // tm, n_tiles),
        in_specs=[pl.BlockSpec((tm, k), lambda i, j: (i, 0)),
                  pl.BlockSpec((1, k, PROJ_TN), lambda i, j: (layer, 0, src_tile(j))),
                  pl.BlockSpec((tm, LANES), lambda i, j: (i % t_blocks, 0)),
                  pl.BlockSpec((tm, LANES), lambda i, j: (i % t_blocks, 0))],
        out_specs=pl.BlockSpec((tm, PROJ_TN), lambda i, j: (i, j)),
        out_shape=jax.ShapeDtypeStruct((m, n_tiles * PROJ_TN), BF16),
        compiler_params=_params(("parallel", "arbitrary")),
        name="proj",
    )(h, w3d, cosf, sinf)


def _proj_b_kernel(a_ref, w_ref, wg_ref, cos_ref, sin_ref, o_ref):
    a = a_ref[...]
    acc = _dot(a, w_ref[0].astype(BF16))
    for c in range(B_GATE):
        xc = acc[:, c * LANES:(c + 1) * LANES]
        o_ref[c] = _rope(xc, cos_ref[...], sin_ref[...]) if c < NSA_KV_HEADS else xc
    o_ref[B_GATE] = _dot(a, wg_ref[0].astype(BF16))


def _proj_b(h, w3d, layer, cosf, sinf, seq):
    m, k = h.shape
    tm = 1024
    t_blocks = seq // tm
    return pl.pallas_call(
        _proj_b_kernel,
        grid=(m // tm,),
        in_specs=[pl.BlockSpec((tm, k), lambda i: (i, 0)),
                  pl.BlockSpec((1, k, 2 * KV_W), lambda i: (layer, 0, W_KCMP // (2 * KV_W))),
                  pl.BlockSpec((1, k, LANES), lambda i: (layer, 0, W_GATE // LANES)),
                  pl.BlockSpec((tm, LANES), lambda i: (i % t_blocks, 0)),
                  pl.BlockSpec((tm, LANES), lambda i: (i % t_blocks, 0))],
        out_specs=pl.BlockSpec((B_CHUNKS, tm, LANES), lambda i: (0, i, 0)),
        out_shape=jax.ShapeDtypeStruct((B_CHUNKS, m, LANES), F32),
        compiler_params=_params(("parallel",)),
        name="proj_b",
    )(h, w3d, w3d, cosf, sinf)


def _compress_kernel(x_ref, pe_ref, w1_ref, w2_ref, o_ref):
    blk = (x_ref[...] + pe_ref[0]).astype(BF16)
    hid = _dot(blk, w1_ref[0])
    hid = hid * jax.nn.sigmoid(hid)
    o_ref[0] = _dot(hid.astype(BF16), w2_ref[0])


def _compress(x_kv, pe, w1, w2):
    rows = x_kv.shape[0] // 2
    kdim = x_kv.shape[1]
    hidden = w1.shape[-1]
    return pl.pallas_call(
        _compress_kernel,
        grid=(2,),
        in_specs=[pl.BlockSpec((rows, kdim), lambda i: (i, 0)),
                  pl.BlockSpec((1, 1, kdim), lambda i: (i, 0, 0)),
                  pl.BlockSpec((1, kdim, hidden), lambda i: (i, 0, 0)),
                  pl.BlockSpec((1, hidden, HEAD_DIM), lambda i: (i, 0, 0))],
        out_specs=pl.BlockSpec((1, rows, HEAD_DIM), lambda i: (i, 0, 0)),
        out_shape=jax.ShapeDtypeStruct((2, rows, HEAD_DIM), F32),
        compiler_params=_params(("parallel",)),
        name="nsa_compress",
    )(x_kv, pe, w1, w2)


NSA_QB = 256
NSA_KC = 128
NSA_CW = 512
NSA_WIN_KEYS = NSA_WINDOW + NSA_QB
NB_PAD = 128


def _nsa_kernel(q_ref, ks_ref, kw_ref, vs_ref, vw_ref, kc_ref, vct_ref, gate_ref, o_ref,
                vst_ref, vwt_ref, val_ref, sel_ref, gt_ref, acc_ref, base_ref, *, seq):
    i = pl.program_id(2)
    qb = NSA_QB
    n_blocks = seq // NSA_BLOCK
    q0 = pl.multiple_of(i * qb, qb)

    @pl.when(i == 0)
    def _():
        for c in range(seq // NSA_KC):
            rows = slice(c * NSA_KC, (c + 1) * NSA_KC)
            vst_ref[c] = vs_ref[rows, :].astype(F32).T.astype(BF16)
            vwt_ref[c] = vw_ref[rows, :].astype(F32).T.astype(BF16)

    qs = [q_ref[:, r * LANES:(r + 1) * LANES] for r in range(NSA_GROUP)]
    t_row = i * qb + lax.broadcasted_iota(jnp.int32, (1, qb), 1)
    sub_n = lax.broadcasted_iota(jnp.int32, (NB_PAD, qb), 0)
    t_full = i * qb + lax.broadcasted_iota(jnp.int32, (NB_PAD, qb), 1)

    cmp_ok = (sub_n * NSA_BLOCK + (NSA_BLOCK - 1) <= t_full) & (sub_n < n_blocks)
    any_ok = jnp.where(t_row >= NSA_BLOCK - 1, 1.0, 0.0)
    kc = kc_ref[0]
    vct = vct_ref[0]
    imp = jnp.zeros((NB_PAD, qb), F32)
    o_cmp = []
    for r in range(NSA_GROUP):
        s = jnp.where(cmp_ok, _nt_dot(kc, qs[r]), NEG)
        e = jnp.exp2(s - jnp.max(s, axis=0, keepdims=True))
        p = e / jnp.sum(e, axis=0, keepdims=True) * any_ok
        imp = imp + p
        o_cmp.append(_dot(vct, p.astype(BF16)))

    cur = jnp.right_shift(t_full, 6)
    started = sub_n <= cur
    forced = (sub_n == 0) | (sub_n == cur) | (sub_n == cur - 1)
    val = jnp.where(forced, NSA_FORCED_SCORE, jnp.where(started, imp, -1.0))
    val_ref[...] = val
    val_n = val[:n_blocks]
    sub_nn = lax.broadcasted_iota(jnp.int32, (n_blocks, qb), 0)
    rank = jnp.zeros((n_blocks, qb), F32)
    for n2 in range(n_blocks):
        row = val_ref[n2:n2 + 1, :]
        rank = rank + jnp.where(sub_nn > n2, jnp.where(row >= val_n, 1.0, 0.0),
                                jnp.where(row > val_n, 1.0, 0.0))
    sel_ref[0:n_blocks, :] = jnp.where(rank < NSA_TOPN, 1.0, 0.0)

    per_step = NSA_CW // NSA_BLOCK
    d_cw = (lax.broadcasted_iota(jnp.int32, (NSA_CW, qb), 0)
            - lax.broadcasted_iota(jnp.int32, (NSA_CW, qb), 1))

    def slc_scores(cw):
        k0 = cw * NSA_CW if isinstance(cw, int) else pl.multiple_of(cw * NSA_CW, NSA_CW)
        k_chunk = ks_ref[pl.ds(k0, NSA_CW), :]
        picked = jnp.concatenate(
            [jnp.broadcast_to(sel_ref[pl.ds(per_step * cw + b, 1), :], (NSA_BLOCK, qb))
             for b in range(per_step)], axis=0)
        mask = jnp.where(d_cw <= q0 - k0, picked, 0.0) > 0.5
        return [jnp.where(mask, _nt_dot(k_chunk, qs[r]), NEG) for r in range(NSA_GROUP)]

    def pv(vt_ref, first_chunk, p, n_chunks):
        out = None
        for j in range(n_chunks):
            part = _dot(vt_ref[first_chunk + j], p[j * NSA_KC:(j + 1) * NSA_KC].astype(BF16))
            out = part if out is None else out + part
        return out

    ms, ls = [], []
    for r, s in enumerate(slc_scores(0)):
        m = jnp.max(s, axis=0, keepdims=True)
        p = jnp.exp2(s - m)
        ms.append(m)
        ls.append(jnp.sum(p, axis=0, keepdims=True))
        acc_ref[r] = pv(vst_ref, 0, p, NSA_CW // NSA_KC)

    w_start = pl.multiple_of(jnp.maximum(q0 - NSA_WINDOW, 0), NSA_KC)
    delta = ((q0 - w_start) + lax.broadcasted_iota(jnp.int32, (NSA_WIN_KEYS, qb), 1)
             - lax.broadcasted_iota(jnp.int32, (NSA_WIN_KEYS, qb), 0))
    in_win = (delta >= 0) & (delta < NSA_WINDOW)
    k_win = kw_ref[pl.ds(w_start, NSA_WIN_KEYS), :]
    first_chunk = jnp.right_shift(w_start, NSA_KC.bit_length() - 1)
    gt_ref[...] = jax.nn.sigmoid(gate_ref[0].T)
    g_row = pl.program_id(1) * (NSA_GROUP * 3)
    for r in range(NSA_GROUP):
        s = jnp.where(in_win, _nt_dot(k_win, qs[r]), NEG)
        p = jnp.exp2(s - jnp.max(s, axis=0, keepdims=True))
        l_win = jnp.sum(p, axis=0, keepdims=True)
        o_win = pv(vwt_ref, first_chunk, p, NSA_WIN_KEYS // NSA_KC)
        g_cmp = gt_ref[pl.ds(g_row + 3 * r, 1), :]
        g_win = gt_ref[pl.ds(g_row + 3 * r + 2, 1), :]
        base_ref[r] = g_cmp * o_cmp[r] + (g_win / l_win) * o_win

    def slc_body(cw, carry):
        ms, ls = carry
        new_m, new_l = [], []
        for r, s in enumerate(slc_scores(cw)):
            m_new = jnp.maximum(ms[r], jnp.max(s, axis=0, keepdims=True))
            alpha = jnp.exp2(ms[r] - m_new)
            p = jnp.exp2(s - m_new)
            new_m.append(m_new)
            new_l.append(alpha * ls[r] + jnp.sum(p, axis=0, keepdims=True))
            acc_ref[r] = alpha * acc_ref[r] + pv(vst_ref, cw * (NSA_CW // NSA_KC), p, NSA_CW // NSA_KC)
        return tuple(new_m), tuple(new_l)

    n_steps = jnp.right_shift(q0 + (qb + NSA_CW - 1), NSA_CW.bit_length() - 1)
    _, ls = lax.fori_loop(1, n_steps, slc_body, (tuple(ms), tuple(ls)))

    for r in range(NSA_GROUP):
        g_slc = gt_ref[pl.ds(g_row + 3 * r + 1, 1), :]
        o = base_ref[r] + (g_slc / ls[r]) * acc_ref[r]
        o_ref[:, r * LANES:(r + 1) * LANES] = o.T.astype(o_ref.dtype)


def _nsa_attention(proj_n, kc, vct, proj_b, batch, seq):
    qb = NSA_QB
    nq = seq // qb
    n_chunks = seq // NSA_KC
    gw = NSA_GROUP * LANES
    kv_spec = lambda base: pl.BlockSpec((seq, LANES), lambda b, g, i: (b, base + g))
    return pl.pallas_call(
        functools.partial(_nsa_kernel, seq=seq),
        grid=(batch, NSA_KV_HEADS, nq),
        in_specs=[
            pl.BlockSpec((qb, gw), lambda b, g, i: (b * nq + i, g)),
            kv_spec(CB_KSLC), kv_spec(CB_KWIN), kv_spec(CB_VSLC), kv_spec(CB_VWIN),
            pl.BlockSpec((1, NB_PAD, LANES), lambda b, g, i: (g * batch + b, 0, 0)),
            pl.BlockSpec((1, LANES, NB_PAD), lambda b, g, i: (g * batch + b, 0, 0)),
            pl.BlockSpec((1, qb, LANES), lambda b, g, i: (B_GATE, b * nq + i, 0)),
        ],
        out_specs=pl.BlockSpec((qb, gw), lambda b, g, i: (b * nq + i, g)),
        out_shape=jax.ShapeDtypeStruct((batch * seq, NSA_HEADS * LANES), BF16),
        scratch_shapes=[
            pltpu.VMEM((n_chunks, LANES, NSA_KC), BF16),
            pltpu.VMEM((n_chunks, LANES, NSA_KC), BF16),
            pltpu.VMEM((NB_PAD, qb), F32),
            pltpu.VMEM((NB_PAD, qb), F32),
            pltpu.VMEM((LANES, qb), F32),
            pltpu.VMEM((NSA_GROUP, LANES, qb), F32),
            pltpu.VMEM((NSA_GROUP, LANES, qb), F32),
        ],
        compiler_params=_params(("arbitrary", "arbitrary", "arbitrary")),
        name="nsa_attention",
    )(proj_n, proj_n, proj_n, proj_n, proj_n, kc, vct, proj_b)


DIL_QB = 256


def _dil_bias(window, dil):
    rows = jnp.arange(DIL_QB, dtype=jnp.int32)[:, None]
    cols = jnp.arange(window + DIL_QB, dtype=jnp.int32)[None, :]
    delta = window + rows - cols
    ok = (delta >= 0) & (delta <= window) & (delta % dil == 0)
    return jnp.where(ok, 0.0, NEG).astype(F32)


def _dil_kernel(q0_ref, q1_ref, q2_ref, k0_ref, k1_ref, k2_ref, v0_ref, v1_ref, v2_ref,
                b0_ref, b1_ref, b2_ref, o0_ref, o1_ref, o2_ref,
                kp0_ref, kp1_ref, kp2_ref, vp0_ref, vp1_ref, vp2_ref, *, seq):
    i = pl.program_id(2)
    qs = (q0_ref, q1_ref, q2_ref)
    ks = (k0_ref, k1_ref, k2_ref)
    vs = (v0_ref, v1_ref, v2_ref)
    biases = (b0_ref, b1_ref, b2_ref)
    kps = (kp0_ref, kp1_ref, kp2_ref)
    vps = (vp0_ref, vp1_ref, vp2_ref)
    outs = (o0_ref, o1_ref, o2_ref)

    @pl.when(i == 0)
    def _():
        for p, (window, _) in enumerate(DIL_PAIRS):
            kps[p][0:window, :] = jnp.zeros((window, LANES), BF16)
            vps[p][0:window, :] = jnp.zeros((window, LANES), BF16)
            kps[p][window:window + seq, :] = ks[p][...]
            vps[p][window:window + seq, :] = vs[p][...]

    q_start = pl.multiple_of(i * DIL_QB, DIL_QB)
    accs, lses = [], []
    for p, (window, _) in enumerate(DIL_PAIRS):
        width = window + DIL_QB
        k_win = kps[p][pl.ds(q_start, width), :]
        v_win = vps[p][pl.ds(q_start, width), :]
        col = lax.broadcasted_iota(jnp.int32, (1, width), 1)
        in_seq = jnp.where(col >= window - q_start, 0.0, NEG)
        s = _nt_dot(qs[p][...], k_win) + biases[p][...] + in_seq
        m = jnp.max(s, axis=1, keepdims=True)
        e = jnp.exp2(s - m)
        l = jnp.sum(e, axis=1, keepdims=True)
        accs.append(_dot(e.astype(BF16), v_win) / l)
        lses.append(m + jnp.log2(l))

    top = jnp.maximum(jnp.maximum(lses[0], lses[1]), lses[2])
    ws = [jnp.exp2(x - top) for x in lses]
    inv = 1.0 / (ws[0] + ws[1] + ws[2])
    for p in range(len(DIL_PAIRS)):
        outs[p][...] = (accs[p] * (ws[p] * inv)).astype(outs[p].dtype)


def _dil_attention(proj_d, batch, seq):
    nq = seq // DIL_QB
    hp = DIL_HEADS_PER_PAIR

    def q_spec(p):
        return pl.BlockSpec((DIL_QB, LANES), lambda b, h, i: (b * nq + i, CB_DQ + p * hp + h))

    def kv_spec(base, p):
        return pl.BlockSpec((seq, LANES), lambda b, h, i: (b, base + p * hp + h))

    def bias_spec(p):
        return pl.BlockSpec((DIL_QB, DIL_PAIRS[p][0] + DIL_QB), lambda b, h, i: (0, 0))

    out_spec = pl.BlockSpec((DIL_QB, LANES), lambda b, h, i: (b * nq + i, h))
    out_shape = jax.ShapeDtypeStruct((batch * seq, hp * LANES), BF16)
    staged = [pltpu.VMEM((w + seq, LANES), BF16) for w, _ in DIL_PAIRS]
    return pl.pallas_call(
        functools.partial(_dil_kernel, seq=seq),
        grid=(batch, hp, nq),
        in_specs=[q_spec(0), q_spec(1), q_spec(2),
                  kv_spec(CB_DK, 0), kv_spec(CB_DK, 1), kv_spec(CB_DK, 2),
                  kv_spec(CB_DV, 0), kv_spec(CB_DV, 1), kv_spec(CB_DV, 2),
                  bias_spec(0), bias_spec(1), bias_spec(2)],
        out_specs=[out_spec, out_spec, out_spec],
        out_shape=[out_shape, out_shape, out_shape],
        scratch_shapes=staged + staged,
        compiler_params=_params(("arbitrary", "arbitrary", "arbitrary")),
        name="dilated_attention",
    )(*([proj_d] * 9), *[_dil_bias(w, d) for w, d in DIL_PAIRS])


SB_QB = 256
SB_PAST = 256
SB_KB = 128
SB_DONE = -104.0


def _sb_suffix(n):
    j = jnp.arange(n, dtype=jnp.int32)[:, None]
    s = jnp.arange(n, dtype=jnp.int32)[None, :]
    return jnp.where(j > s, 1.0, 0.0).astype(BF16)


SB_HP = 2


def _sb_kernel(q_ref, k_ref, v_ref, suffix_ref, o_ref):
    i = pl.program_id(2)
    width = SB_PAST + SB_QB

    def tile(q, k_chunk, v_chunk, suffix, carry_sum, before):
        z = _nt_dot(q, k_chunk)
        soft = jnp.log(1.0 + jnp.exp(-jnp.abs(z)))
        log_beta = jnp.minimum(z, 0.0) - soft
        log_1m = -jnp.maximum(z, 0.0) - soft
        if before is not None:
            log_1m = jnp.where(before, log_1m, 0.0)
        hi = log_1m.astype(BF16)
        lo = (log_1m - hi.astype(F32)).astype(BF16)
        logit = log_beta + (_dot(hi, suffix) + _dot(lo, suffix)) + carry_sum
        if before is not None:
            logit = jnp.where(before, logit, NEG)
        contrib = _dot(jnp.exp(logit).astype(BF16), v_chunk)
        return contrib, carry_sum + jnp.sum(log_1m, axis=1, keepdims=True)

    q_start = i * SB_QB
    w_start = pl.multiple_of(jnp.maximum(q_start - SB_PAST, 0), SB_QB)
    offs = (lax.broadcasted_iota(jnp.int32, (SB_QB, width), 1)
            - lax.broadcasted_iota(jnp.int32, (SB_QB, width), 0))
    before = offs < q_start - w_start
    heads = [slice(h * LANES, (h + 1) * LANES) for h in range(SB_HP)]
    qs = [q_ref[:, sl] for sl in heads]
    state = [tile(qs[h], k_ref[pl.ds(w_start, width), heads[h]], v_ref[pl.ds(w_start, width), heads[h]],
                  suffix_ref[...], jnp.zeros((SB_QB, 1), F32), before) for h in range(SB_HP)]
    accs = tuple(s[0] for s in state)
    csums = tuple(s[1] for s in state)

    n_rest = jnp.right_shift(w_start, SB_KB.bit_length() - 1)
    suffix_kb = suffix_ref[0:SB_KB, 0:SB_KB]

    def live_of(csums):
        top = csums[0]
        for c in csums[1:]:
            top = jnp.maximum(top, c)
        return jnp.max(top)

    def cond(carry):
        c, _, _, live = carry
        return (c < n_rest) & (live > SB_DONE)

    def body(carry):
        c, accs, csums, _ = carry
        k_start = pl.multiple_of((n_rest - 1 - c) * SB_KB, SB_KB)
        new_a, new_c = [], []
        for h in range(SB_HP):
            contrib, cs = tile(qs[h], k_ref[pl.ds(k_start, SB_KB), heads[h]],
                               v_ref[pl.ds(k_start, SB_KB), heads[h]], suffix_kb, csums[h], None)
            new_a.append(accs[h] + contrib)
            new_c.append(cs)
        return c + 1, tuple(new_a), tuple(new_c), live_of(new_c)

    _, accs, _, _ = lax.while_loop(cond, body, (jnp.int32(0), accs, csums, live_of(csums)))
    for h in range(SB_HP):
        o_ref[:, heads[h]] = accs[h].astype(o_ref.dtype)


def _sb_attention(proj_d, batch, seq):
    nq = seq // SB_QB
    width = SB_PAST + SB_QB
    hw = SB_HP * LANES
    return pl.pallas_call(
        _sb_kernel,
        grid=(batch, SB_HEADS // SB_HP, nq),
        in_specs=[pl.BlockSpec((SB_QB, hw), lambda b, h, i: (b * nq + i, CB_SQ // SB_HP + h)),
                  pl.BlockSpec((seq, hw), lambda b, h, i: (b, CB_SK // SB_HP + h)),
                  pl.BlockSpec((seq, hw), lambda b, h, i: (b, CB_SV // SB_HP + h)),
                  pl.BlockSpec((width, width), lambda b, h, i: (0, 0))],
        out_specs=pl.BlockSpec((SB_QB, hw), lambda b, h, i: (b * nq + i, h)),
        out_shape=jax.ShapeDtypeStruct((batch * seq, SB_HEADS * LANES), BF16),
        compiler_params=_params(("parallel", "parallel", "arbitrary")),
        name="stick_breaking_attention",
    )(proj_d, proj_d, proj_d, _sb_suffix(width))


OUT_PIECES = (NSA_HEADS * LANES,) + (DIL_HEADS_PER_PAIR * LANES,) * 3 + (SB_HEADS * LANES,)


def _out_proj_kernel(a0, a1, a2, a3, a4, w_ref, x_ref, o_ref):
    acc = x_ref[...]
    off = 0
    for a_ref, width in zip((a0, a1, a2, a3, a4), OUT_PIECES):
        acc = acc + _dot(a_ref[...], w_ref[0, off:off + width, :].astype(BF16))
        off += width
    o_ref[...] = acc


def _out_proj(pieces, w3d, layer, x2d):
    m, n = x2d.shape
    tm, tn = 1024, 512
    a_specs = [pl.BlockSpec((tm, width), lambda i, j: (i, 0)) for width in OUT_PIECES]
    return pl.pallas_call(
        _out_proj_kernel,
        grid=(m // tm, n // tn),
        in_specs=a_specs + [pl.BlockSpec((1, w3d.shape[1], tn), lambda i, j: (layer, 0, j)),
                            pl.BlockSpec((tm, tn), lambda i, j: (i, j))],
        out_specs=pl.BlockSpec((tm, tn), lambda i, j: (i, j)),
        out_shape=jax.ShapeDtypeStruct((m, n), F32),
        compiler_params=_params(("parallel", "arbitrary")),
        name="out_proj",
    )(*pieces, w3d, x2d)


def _mlp_up_kernel(a_ref, w_ref, o_ref):
    r = jnp.maximum(_dot(a_ref[...], w_ref[0].astype(BF16)), 0.0)
    o_ref[...] = (r * r).astype(o_ref.dtype)


def _mlp_up(h, w3d, layer):
    m, k = h.shape
    n = w3d.shape[2]
    tm, tn = 2048, 256
    return pl.pallas_call(
        _mlp_up_kernel,
        grid=(m // tm, n // tn),
        in_specs=[pl.BlockSpec((tm, k), lambda i, j: (i, 0)),
                  pl.BlockSpec((1, k, tn), lambda i, j: (layer, 0, j))],
        out_specs=pl.BlockSpec((tm, tn), lambda i, j: (i, j)),
        out_shape=jax.ShapeDtypeStruct((m, n), BF16),
        compiler_params=_params(("parallel", "arbitrary")),
        name="mlp_up",
    )(h, w3d)


def _mlp_down_kernel(a_ref, w_ref, x_ref, o_ref):
    @pl.when(pl.program_id(2) == 0)
    def _():
        o_ref[...] = x_ref[...]

    o_ref[...] += _dot(a_ref[...], w_ref[0])


def _mlp_down(a, w3d, layer, x2d):
    m, kdim = a.shape
    n = w3d.shape[2]
    tm, tn, tk = 1024, 1024, 2048
    return pl.pallas_call(
        _mlp_down_kernel,
        grid=(m // tm, n // tn, kdim // tk),
        in_specs=[pl.BlockSpec((tm, tk), lambda i, j, k: (i, k)),
                  pl.BlockSpec((1, tk, tn), lambda i, j, k: (layer, k, j)),
                  pl.BlockSpec((tm, tn), lambda i, j, k: (i, j))],
        out_specs=pl.BlockSpec((tm, tn), lambda i, j, k: (i, j)),
        out_shape=jax.ShapeDtypeStruct((m, n), F32),
        compiler_params=_params(("parallel", "parallel", "arbitrary")),
        name="mlp_down",
    )(a, w3d, x2d)


def _rope_tables(seq):
    inv = 1.0 / (ROPE_THETA ** (jnp.arange(0, HEAD_DIM, 2, dtype=F32) / HEAD_DIM))
    ang = jnp.arange(seq, dtype=F32)[:, None] * inv[None, :]
    cos, sin = jnp.cos(ang), jnp.sin(ang)
    return jnp.concatenate([cos, cos], axis=-1), jnp.concatenate([-sin, sin], axis=-1)


def _nsa_mixer(proj_n, proj_b, batch, seq, pe_k, pe_v, w_ck1, w_ck2, w_cv1, w_cv2):
    nb = seq // NSA_BLOCK
    groups = NSA_KV_HEADS
    x_kv = proj_b[:B_GATE].reshape(2 * groups * batch * nb, NSA_BLOCK * HEAD_DIM)
    pe = jnp.stack([pe_k.reshape(1, -1), pe_v.reshape(1, -1)])
    w1 = jnp.stack([w_ck1, w_cv1]).astype(BF16)
    w2 = jnp.stack([w_ck2, w_cv2]).astype(BF16)
    cmp_kv = _compress(x_kv, pe, w1, w2)
    cmp_kv = cmp_kv.reshape(2, groups * batch, nb, HEAD_DIM)
    cmp_kv = jnp.pad(cmp_kv, ((0, 0), (0, 0), (0, NB_PAD - nb), (0, 0))).astype(BF16)
    kc = cmp_kv[0]
    vct = cmp_kv[1].transpose(0, 2, 1)
    return _nsa_attention(proj_n, kc, vct, proj_b, batch, seq)


def _layer(x2d, batch, seq, cosf, sinf, layer, w_in, w_tail, g_mix, w_out, g_mlp, w_up, w_down,
           pe_k, pe_v, w_ck1, w_ck2, w_cv1, w_cv2):
    h = _rmsnorm(x2d, g_mix, BF16)
    proj_n = _proj(h, w_in, layer, lambda j: jnp.where(j >= 3, j + 1, j), N_COLS // PROJ_TN, 3, 2,
                   ((0, 2, SCALE_LOG2E),), cosf, sinf, seq)
    q_tiles = DIL_HEADS * LANES // PROJ_TN
    sq_tile = CB_SQ * LANES // PROJ_TN
    proj_d = _proj(h, w_tail, layer, lambda j: j, A_COLS // PROJ_TN, 2 * q_tiles, 0,
                   ((0, q_tiles - 1, SCALE_LOG2E), (sq_tile, sq_tile + SB_HEADS * LANES // PROJ_TN - 1, SCALE)),
                   cosf, sinf, seq)
    proj_b = _proj_b(h, w_in, layer, cosf, sinf, seq)

    o_a = _nsa_mixer(proj_n, proj_b, batch, seq, pe_k, pe_v, w_ck1, w_ck2, w_cv1, w_cv2)
    o_b = _dil_attention(proj_d, batch, seq)
    o_c = _sb_attention(proj_d, batch, seq)

    x2d = _out_proj([o_a, *o_b, o_c], w_out, layer, x2d)
    h2 = _rmsnorm(x2d, g_mlp, BF16)
    up = _mlp_up(h2, w_up, layer)
    return _mlp_down(up, w_down, layer, x2d)


def kernel(x, w_in, w_out, norm_mix, norm_mlp, w_up, w_down, nsa_pe_k, nsa_pe_v,
           nsa_w_ck1, nsa_w_ck2, nsa_w_cv1, nsa_w_cv2, final_norm):
    batch, seq, d = x.shape
    cosf, sinf = _rope_tables(seq)
    depth = w_in.shape[0]
    w_head = w_in.astype(BF16)
    w_tail = w_head[:, :, W_TAIL:]
    w_down = w_down.astype(BF16)
    x2d = x.reshape(batch * seq, d)
    for layer in range(depth):
        x2d = _layer(x2d, batch, seq, cosf, sinf, layer, w_head, w_tail, norm_mix[layer], w_out,
                     norm_mlp[layer], w_up, w_down, nsa_pe_k[layer], nsa_pe_v[layer],
                     nsa_w_ck1[layer], nsa_w_ck2[layer], nsa_w_cv1[layer], nsa_w_cv2[layer])
    return _rmsnorm(x2d, final_norm, x.dtype).reshape(batch, seq, d)
```

```python
import functools

import jax
import jax.numpy as jnp
from jax import lax
from jax.experimental import pallas as pl
from jax.experimental.pallas import tpu as pltpu

F32 = jnp.float32
BF16 = jnp.bfloat16

D_MODEL = 4096
HEAD_DIM = 128
ROPE_THETA = 10000.0
EPS = 1e-6
NEG = -1e30
NSA_HEADS = 12
NSA_KV_HEADS = 2
NSA_GROUP = NSA_HEADS // NSA_KV_HEADS
NSA_BLOCK = 64
NSA_TOPN = 16
NSA_WINDOW = 512
NSA_FORCED_SCORE = NSA_GROUP + 1.0
DIL_PAIRS = ((128, 1), (512, 4), (2048, 16))
DIL_HEADS_PER_PAIR = 4
DIL_HEADS = DIL_HEADS_PER_PAIR * len(DIL_PAIRS)
SB_HEADS = 8
D_FF = 4 * D_MODEL
SCALE = HEAD_DIM ** -0.5
SCALE_LOG2E = SCALE * 1.4426950408889634

LANES = 128
VMEM_LIMIT = 56 * 1024 * 1024

KV_W = NSA_KV_HEADS * HEAD_DIM
W_KCMP = NSA_HEADS * HEAD_DIM
W_GATE = W_KCMP + 6 * KV_W
N_GATE = NSA_HEADS * 3
W_TAIL = W_GATE + N_GATE
PROJ_TN = 512
N_COLS = 20 * LANES
CB_NSA_Q, CB_KSLC, CB_VSLC, CB_KWIN, CB_VWIN = 0, 12, 14, 16, 18
A_COLS = 60 * LANES
CB_DQ, CB_DK, CB_DV = 0, 12, 24
CB_SQ, CB_SK, CB_SV = 36, 44, 52
B_CHUNKS = 5
B_GATE = 4


def _params(sem):
    return pltpu.CompilerParams(dimension_semantics=sem, vmem_limit_bytes=VMEM_LIMIT)


def _nt_dot(a, b):
    return lax.dot_general(a, b, (((1,), (1,)), ((), ())), preferred_element_type=F32)


def _dot(a, b):
    return jnp.dot(a, b, preferred_element_type=F32)


def _rmsnorm_kernel(x_ref, g_ref, o_ref):
    x = x_ref[...]
    ms = jnp.mean(x * x, axis=-1, keepdims=True)
    o_ref[...] = (x * lax.rsqrt(ms + EPS) * g_ref[...]).astype(o_ref.dtype)


def _rmsnorm(x2d, g, out_dtype):
    m, d = x2d.shape
    tm = 256
    return pl.pallas_call(
        _rmsnorm_kernel,
        grid=(m // tm,),
        in_specs=[pl.BlockSpec((tm, d), lambda i: (i, 0)),
                  pl.BlockSpec((1, d), lambda i: (0, 0))],
        out_specs=pl.BlockSpec((tm, d), lambda i: (i, 0)),
        out_shape=jax.ShapeDtypeStruct((m, d), out_dtype),
        compiler_params=_params(("parallel",)),
        name="rmsnorm",
    )(x2d, g.reshape(1, d))


def _rope(xc, cos, sin):
    return xc * cos + pltpu.roll(xc, HEAD_DIM // 2, 1) * sin


def _proj_kernel(a_ref, w_ref, cos_ref, sin_ref, o_ref, *, n_full, n_half, q_scales):
    j = pl.program_id(1)
    acc = _nt_dot(a_ref[...], w_ref[0].astype(BF16))
    factor = jnp.float32(1.0)
    for lo, hi, f in q_scales:
        factor = jnp.where((j >= lo) & (j <= hi), jnp.float32(f), factor)

    n_roped = jnp.where(j < n_full, PROJ_TN // LANES, jnp.where(j < n_full + n_half, NSA_KV_HEADS, 0))
    cos_f = cos_ref[...] * factor
    sin_f = sin_ref[...] * factor
    for c in range(PROJ_TN // LANES):
        sl = slice(c * LANES, (c + 1) * LANES)
        roped = n_roped > c
        o_ref[:, sl] = _rope(acc[:, sl], jnp.where(roped, cos_f, factor),
                             jnp.where(roped, sin_f, 0.0)).astype(o_ref.dtype)


def _proj(h, w3d, layer, src_tile, n_tiles, n_full, n_half, q_scales, cosf, sinf, seq):
    m, k = h.shape
    tm = 1024
    t_blocks = seq // tm
    return pl.pallas_call(
        functools.partial(_proj_kernel, n_full=n_full, n_half=n_half, q_scales=q_scales),
        grid=(m // tm, n_tiles),
        in_specs=[pl.BlockSpec((tm, k), lambda i, j: (i, 0)),
                  pl.BlockSpec((1, PROJ_TN, k), lambda i, j: (layer, src_tile(j), 0)),
                  pl.BlockSpec((tm, LANES), lambda i, j: (i % t_blocks, 0)),
                  pl.BlockSpec((tm, LANES), lambda i, j: (i % t_blocks, 0))],
        out_specs=pl.BlockSpec((tm, PROJ_TN), lambda i, j: (i, j)),
        out_shape=jax.ShapeDtypeStruct((m, n_tiles * PROJ_TN), BF16),
        compiler_params=_params(("parallel", "arbitrary")),
        name="proj",
    )(h, w3d, cosf, sinf)


def _proj_b_kernel(a_ref, w_ref, wg_ref, cos_ref, sin_ref, o_ref):
    a = a_ref[...]
    acc = _nt_dot(a, w_ref[0].astype(BF16))
    for c in range(B_GATE):
        xc = acc[:, c * LANES:(c + 1) * LANES]
        o_ref[c] = _rope(xc, cos_ref[...], sin_ref[...]) if c < NSA_KV_HEADS else xc
    o_ref[B_GATE] = _nt_dot(a, wg_ref[0].astype(BF16))


def _proj_b(h, w3d, layer, cosf, sinf, seq):
    m, k = h.shape
    tm = 1024
    t_blocks = seq // tm
    return pl.pallas_call(
        _proj_b_kernel,
        grid=(m // tm,),
        in_specs=[pl.BlockSpec((tm, k), lambda i: (i, 0)),
                  pl.BlockSpec((1, 2 * KV_W, k), lambda i: (layer, W_KCMP // (2 * KV_W), 0)),
                  pl.BlockSpec((1, LANES, k), lambda i: (layer, W_GATE // LANES, 0)),
                  pl.BlockSpec((tm, LANES), lambda i: (i % t_blocks, 0)),
                  pl.BlockSpec((tm, LANES), lambda i: (i % t_blocks, 0))],
        out_specs=pl.BlockSpec((B_CHUNKS, tm, LANES), lambda i: (0, i, 0)),
        out_shape=jax.ShapeDtypeStruct((B_CHUNKS, m, LANES), F32),
        compiler_params=_params(("parallel",)),
        name="proj_b",
    )(h, w3d, w3d, cosf, sinf)


def _compress_kernel(x_ref, pe_ref, w1_ref, w2_ref, o_ref):
    blk = (x_ref[...] + pe_ref[0]).astype(BF16)
    hid = _dot(blk, w1_ref[0])
    hid = hid * jax.nn.sigmoid(hid)
    o_ref[0] = _dot(hid.astype(BF16), w2_ref[0])


def _compress(x_kv, pe, w1, w2):
    rows = x_kv.shape[0] // 2
    kdim = x_kv.shape[1]
    hidden = w1.shape[-1]
    return pl.pallas_call(
        _compress_kernel,
        grid=(2,),
        in_specs=[pl.BlockSpec((rows, kdim), lambda i: (i, 0)),
                  pl.BlockSpec((1, 1, kdim), lambda i: (i, 0, 0)),
                  pl.BlockSpec((1, kdim, hidden), lambda i: (i, 0, 0)),
                  pl.BlockSpec((1, hidden, HEAD_DIM), lambda i: (i, 0, 0))],
        out_specs=pl.BlockSpec((1, rows, HEAD_DIM), lambda i: (i, 0, 0)),
        out_shape=jax.ShapeDtypeStruct((2, rows, HEAD_DIM), F32),
        compiler_params=_params(("parallel",)),
        name="nsa_compress",
    )(x_kv, pe, w1, w2)


NSA_QB = 256
NSA_KC = 128
NSA_CW = 512
NSA_WIN_KEYS = NSA_WINDOW + NSA_QB
NB_PAD = 128


def _nsa_kernel(q_ref, ks_ref, kw_ref, vs_ref, vw_ref, kc_ref, vct_ref, gate_ref, o_ref,
                vst_ref, vwt_ref, val_ref, sel_ref, gt_ref, acc_ref, base_ref, *, seq):
    i = pl.program_id(2)
    qb = NSA_QB
    n_blocks = seq // NSA_BLOCK
    q0 = pl.multiple_of(i * qb, qb)

    @pl.when(i == 0)
    def _():
        for c in range(seq // NSA_KC):
            rows = slice(c * NSA_KC, (c + 1) * NSA_KC)
            vst_ref[c] = vs_ref[rows, :].astype(F32).T.astype(BF16)
            vwt_ref[c] = vw_ref[rows, :].astype(F32).T.astype(BF16)

    qs = [q_ref[:, r * LANES:(r + 1) * LANES] for r in range(NSA_GROUP)]
    t_row = i * qb + lax.broadcasted_iota(jnp.int32, (1, qb), 1)
    sub_n = lax.broadcasted_iota(jnp.int32, (NB_PAD, qb), 0)
    t_full = i * qb + lax.broadcasted_iota(jnp.int32, (NB_PAD, qb), 1)

    cmp_ok = (sub_n * NSA_BLOCK + (NSA_BLOCK - 1) <= t_full) & (sub_n < n_blocks)
    any_ok = jnp.where(t_row >= NSA_BLOCK - 1, 1.0, 0.0)
    kc = kc_ref[0]
    vct = vct_ref[0]
    imp = jnp.zeros((NB_PAD, qb), F32)
    o_cmp = []
    for r in range(NSA_GROUP):
        s = jnp.where(cmp_ok, _nt_dot(kc, qs[r]), NEG)
        e = jnp.exp2(s - jnp.max(s, axis=0, keepdims=True))
        p = e / jnp.sum(e, axis=0, keepdims=True) * any_ok
        imp = imp + p
        o_cmp.append(_dot(vct, p.astype(BF16)))

    cur = jnp.right_shift(t_full, 6)
    started = sub_n <= cur
    forced = (sub_n == 0) | (sub_n == cur) | (sub_n == cur - 1)
    val = jnp.where(forced, NSA_FORCED_SCORE, jnp.where(started, imp, -1.0))
    val_ref[...] = val
    val_n = val[:n_blocks]
    sub_nn = lax.broadcasted_iota(jnp.int32, (n_blocks, qb), 0)
    rank = jnp.zeros((n_blocks, qb), F32)
    for n2 in range(n_blocks):
        row = val_ref[n2:n2 + 1, :]
        rank = rank + jnp.where(sub_nn > n2, jnp.where(row >= val_n, 1.0, 0.0),
                                jnp.where(row > val_n, 1.0, 0.0))
    sel_ref[0:n_blocks, :] = jnp.where(rank < NSA_TOPN, 1.0, 0.0)

    per_step = NSA_CW // NSA_BLOCK
    d_cw = (lax.broadcasted_iota(jnp.int32, (NSA_CW, qb), 0)
            - lax.broadcasted_iota(jnp.int32, (NSA_CW, qb), 1))

    def slc_scores(cw):
        k0 = cw * NSA_CW if isinstance(cw, int) else pl.multiple_of(cw * NSA_CW, NSA_CW)
        k_chunk = ks_ref[pl.ds(k0, NSA_CW), :]
        picked = jnp.concatenate(
            [jnp.broadcast_to(sel_ref[pl.ds(per_step * cw + b, 1), :], (NSA_BLOCK, qb))
             for b in range(per_step)], axis=0)
        mask = jnp.where(d_cw <= q0 - k0, picked, 0.0) > 0.5
        return [jnp.where(mask, _nt_dot(k_chunk, qs[r]), NEG) for r in range(NSA_GROUP)]

    def pv(vt_ref, first_chunk, p, n_chunks):
        out = None
        for j in range(n_chunks):
            part = _dot(vt_ref[first_chunk + j], p[j * NSA_KC:(j + 1) * NSA_KC].astype(BF16))
            out = part if out is None else out + part
        return out

    ms, ls = [], []
    for r, s in enumerate(slc_scores(0)):
        m = jnp.max(s, axis=0, keepdims=True)
        p = jnp.exp2(s - m)
        ms.append(m)
        ls.append(jnp.sum(p, axis=0, keepdims=True))
        acc_ref[r] = pv(vst_ref, 0, p, NSA_CW // NSA_KC)

    w_start = pl.multiple_of(jnp.maximum(q0 - NSA_WINDOW, 0), NSA_KC)
    delta = ((q0 - w_start) + lax.broadcasted_iota(jnp.int32, (NSA_WIN_KEYS, qb), 1)
             - lax.broadcasted_iota(jnp.int32, (NSA_WIN_KEYS, qb), 0))
    in_win = (delta >= 0) & (delta < NSA_WINDOW)
    k_win = kw_ref[pl.ds(w_start, NSA_WIN_KEYS), :]
    first_chunk = jnp.right_shift(w_start, NSA_KC.bit_length() - 1)
    gt_ref[...] = jax.nn.sigmoid(gate_ref[0].T)
    g_row = pl.program_id(1) * (NSA_GROUP * 3)
    for r in range(NSA_GROUP):
        s = jnp.where(in_win, _nt_dot(k_win, qs[r]), NEG)
        p = jnp.exp2(s - jnp.max(s, axis=0, keepdims=True))
        l_win = jnp.sum(p, axis=0, keepdims=True)
        o_win = pv(vwt_ref, first_chunk, p, NSA_WIN_KEYS // NSA_KC)
        g_cmp = gt_ref[pl.ds(g_row + 3 * r, 1), :]
        g_win = gt_ref[pl.ds(g_row + 3 * r + 2, 1), :]
        base_ref[r] = g_cmp * o_cmp[r] + (g_win / l_win) * o_win

    def slc_body(cw, carry):
        ms, ls = carry
        new_m, new_l = [], []
        for r, s in enumerate(slc_scores(cw)):
            m_new = jnp.maximum(ms[r], jnp.max(s, axis=0, keepdims=True))
            alpha = jnp.exp2(ms[r] - m_new)
            p = jnp.exp2(s - m_new)
            new_m.append(m_new)
            new_l.append(alpha * ls[r] + jnp.sum(p, axis=0, keepdims=True))
            acc_ref[r] = alpha * acc_ref[r] + pv(vst_ref, cw * (NSA_CW // NSA_KC), p, NSA_CW // NSA_KC)
        return tuple(new_m), tuple(new_l)

    n_steps = jnp.right_shift(q0 + (qb + NSA_CW - 1), NSA_CW.bit_length() - 1)
    _, ls = lax.fori_loop(1, n_steps, slc_body, (tuple(ms), tuple(ls)))

    for r in range(NSA_GROUP):
        g_slc = gt_ref[pl.ds(g_row + 3 * r + 1, 1), :]
        o = base_ref[r] + (g_slc / ls[r]) * acc_ref[r]
        o_ref[:, r * LANES:(r + 1) * LANES] = o.T.astype(o_ref.dtype)


def _nsa_attention(proj_n, kc, vct, proj_b, batch, seq):
    qb = NSA_QB
    nq = seq // qb
    n_chunks = seq // NSA_KC
    gw = NSA_GROUP * LANES
    kv_spec = lambda base: pl.BlockSpec((seq, LANES), lambda b, g, i: (b, base + g))
    return pl.pallas_call(
        functools.partial(_nsa_kernel, seq=seq),
        grid=(batch, NSA_KV_HEADS, nq),
        in_specs=[
            pl.BlockSpec((qb, gw), lambda b, g, i: (b * nq + i, g)),
            kv_spec(CB_KSLC), kv_spec(CB_KWIN), kv_spec(CB_VSLC), kv_spec(CB_VWIN),
            pl.BlockSpec((1, NB_PAD, LANES), lambda b, g, i: (g * batch + b, 0, 0)),
            pl.BlockSpec((1, LANES, NB_PAD), lambda b, g, i: (g * batch + b, 0, 0)),
            pl.BlockSpec((1, qb, LANES), lambda b, g, i: (B_GATE, b * nq + i, 0)),
        ],
        out_specs=pl.BlockSpec((qb, gw), lambda b, g, i: (b * nq + i, g)),
        out_shape=jax.ShapeDtypeStruct((batch * seq, NSA_HEADS * LANES), BF16),
        scratch_shapes=[
            pltpu.VMEM((n_chunks, LANES, NSA_KC), BF16),
            pltpu.VMEM((n_chunks, LANES, NSA_KC), BF16),
            pltpu.VMEM((NB_PAD, qb), F32),
            pltpu.VMEM((NB_PAD, qb), F32),
            pltpu.VMEM((LANES, qb), F32),
            pltpu.VMEM((NSA_GROUP, LANES, qb), F32),
            pltpu.VMEM((NSA_GROUP, LANES, qb), F32),
        ],
        compiler_params=_params(("arbitrary", "arbitrary", "arbitrary")),
        name="nsa_attention",
    )(proj_n, proj_n, proj_n, proj_n, proj_n, kc, vct, proj_b)


DIL_QB = 256


def _dil_bias(window, dil):
    rows = jnp.arange(DIL_QB, dtype=jnp.int32)[:, None]
    cols = jnp.arange(window + DIL_QB, dtype=jnp.int32)[None, :]
    delta = window + rows - cols
    ok = (delta >= 0) & (delta <= window) & (delta % dil == 0)
    return jnp.where(ok, 0.0, NEG).astype(F32)


def _dil_kernel(q0_ref, q1_ref, q2_ref, k0_ref, k1_ref, k2_ref, v0_ref, v1_ref, v2_ref,
                b0_ref, b1_ref, b2_ref, o0_ref, o1_ref, o2_ref,
                kp0_ref, kp1_ref, kp2_ref, vp0_ref, vp1_ref, vp2_ref, *, seq):
    i = pl.program_id(2)
    qs = (q0_ref, q1_ref, q2_ref)
    ks = (k0_ref, k1_ref, k2_ref)
    vs = (v0_ref, v1_ref, v2_ref)
    biases = (b0_ref, b1_ref, b2_ref)
    kps = (kp0_ref, kp1_ref, kp2_ref)
    vps = (vp0_ref, vp1_ref, vp2_ref)
    outs = (o0_ref, o1_ref, o2_ref)

    @pl.when(i == 0)
    def _():
        for p, (window, _) in enumerate(DIL_PAIRS):
            kps[p][0:window, :] = jnp.zeros((window, LANES), BF16)
            vps[p][0:window, :] = jnp.zeros((window, LANES), BF16)
            kps[p][window:window + seq, :] = ks[p][...]
            vps[p][window:window + seq, :] = vs[p][...]

    q_start = pl.multiple_of(i * DIL_QB, DIL_QB)
    accs, lses = [], []
    for p, (window, _) in enumerate(DIL_PAIRS):
        width = window + DIL_QB
        k_win = kps[p][pl.ds(q_start, width), :]
        v_win = vps[p][pl.ds(q_start, width), :]
        col = lax.broadcasted_iota(jnp.int32, (1, width), 1)
        in_seq = jnp.where(col >= window - q_start, 0.0, NEG)
        s = _nt_dot(qs[p][...], k_win) + biases[p][...] + in_seq
        m = jnp.max(s, axis=1, keepdims=True)
        e = jnp.exp2(s - m)
        l = jnp.sum(e, axis=1, keepdims=True)
        accs.append(_dot(e.astype(BF16), v_win) / l)
        lses.append(m + jnp.log2(l))

    top = jnp.maximum(jnp.maximum(lses[0], lses[1]), lses[2])
    ws = [jnp.exp2(x - top) for x in lses]
    inv = 1.0 / (ws[0] + ws[1] + ws[2])
    for p in range(len(DIL_PAIRS)):
        outs[p][...] = (accs[p] * (ws[p] * inv)).astype(outs[p].dtype)


def _dil_attention(proj_d, batch, seq):
    nq = seq // DIL_QB
    hp = DIL_HEADS_PER_PAIR

    def q_spec(p):
        return pl.BlockSpec((DIL_QB, LANES), lambda b, h, i: (b * nq + i, CB_DQ + p * hp + h))

    def kv_spec(base, p):
        return pl.BlockSpec((seq, LANES), lambda b, h, i: (b, base + p * hp + h))

    def bias_spec(p):
        return pl.BlockSpec((DIL_QB, DIL_PAIRS[p][0] + DIL_QB), lambda b, h, i: (0, 0))

    out_spec = pl.BlockSpec((DIL_QB, LANES), lambda b, h, i: (b * nq + i, h))
    out_shape = jax.ShapeDtypeStruct((batch * seq, hp * LANES), BF16)
    staged = [pltpu.VMEM((w + seq, LANES), BF16) for w, _ in DIL_PAIRS]
    return pl.pallas_call(
        functools.partial(_dil_kernel, seq=seq),
        grid=(batch, hp, nq),
        in_specs=[q_spec(0), q_spec(1), q_spec(2),
                  kv_spec(CB_DK, 0), kv_spec(CB_DK, 1), kv_spec(CB_DK, 2),
                  kv_spec(CB_DV, 0), kv_spec(CB_DV, 1), kv_spec(CB_DV, 2),
                  bias_spec(0), bias_spec(1), bias_spec(2)],
        out_specs=[out_spec, out_spec, out_spec],
        out_shape=[out_shape, out_shape, out_shape],
        scratch_shapes=staged + staged,
        compiler_params=_params(("arbitrary", "arbitrary", "arbitrary")),
        name="dilated_attention",
    )(*([proj_d] * 9), *[_dil_bias(w, d) for w, d in DIL_PAIRS])


SB_QB = 256
SB_PAST = 256
SB_KB = 128
SB_DONE = -104.0


def _sb_suffix(n):
    j = jnp.arange(n, dtype=jnp.int32)[:, None]
    s = jnp.arange(n, dtype=jnp.int32)[None, :]
    return jnp.where(j > s, 1.0, 0.0).astype(BF16)


SB_HP = 2


def _sb_kernel(q_ref, k_ref, v_ref, suffix_ref, o_ref):
    i = pl.program_id(2)
    width = SB_PAST + SB_QB

    def tile(q, k_chunk, v_chunk, suffix, carry_sum, before):
        z = _nt_dot(q, k_chunk)
        soft = jnp.log(1.0 + jnp.exp(-jnp.abs(z)))
        log_beta = jnp.minimum(z, 0.0) - soft
        log_1m = -jnp.maximum(z, 0.0) - soft
        if before is not None:
            log_1m = jnp.where(before, log_1m, 0.0)
        hi = log_1m.astype(BF16)
        lo = (log_1m - hi.astype(F32)).astype(BF16)
        logit = log_beta + (_dot(hi, suffix) + _dot(lo, suffix)) + carry_sum
        if before is not None:
            logit = jnp.where(before, logit, NEG)
        contrib = _dot(jnp.exp(logit).astype(BF16), v_chunk)
        return contrib, carry_sum + jnp.sum(log_1m, axis=1, keepdims=True)

    q_start = i * SB_QB
    w_start = pl.multiple_of(jnp.maximum(q_start - SB_PAST, 0), SB_QB)
    offs = (lax.broadcasted_iota(jnp.int32, (SB_QB, width), 1)
            - lax.broadcasted_iota(jnp.int32, (SB_QB, width), 0))
    before = offs < q_start - w_start
    heads = [slice(h * LANES, (h + 1) * LANES) for h in range(SB_HP)]
    qs = [q_ref[:, sl] for sl in heads]
    state = [tile(qs[h], k_ref[pl.ds(w_start, width), heads[h]], v_ref[pl.ds(w_start, width), heads[h]],
                  suffix_ref[...], jnp.zeros((SB_QB, 1), F32), before) for h in range(SB_HP)]
    accs = tuple(s[0] for s in state)
    csums = tuple(s[1] for s in state)

    n_rest = jnp.right_shift(w_start, SB_KB.bit_length() - 1)
    suffix_kb = suffix_ref[0:SB_KB, 0:SB_KB]

    def live_of(csums):
        top = csums[0]
        for c in csums[1:]:
            top = jnp.maximum(top, c)
        return jnp.max(top)

    def cond(carry):
        c, _, _, live = carry
        return (c < n_rest) & (live > SB_DONE)

    def body(carry):
        c, accs, csums, _ = carry
        k_start = pl.multiple_of((n_rest - 1 - c) * SB_KB, SB_KB)
        new_a, new_c = [], []
        for h in range(SB_HP):
            contrib, cs = tile(qs[h], k_ref[pl.ds(k_start, SB_KB), heads[h]],
                               v_ref[pl.ds(k_start, SB_KB), heads[h]], suffix_kb, csums[h], None)
            new_a.append(accs[h] + contrib)
            new_c.append(cs)
        return c + 1, tuple(new_a), tuple(new_c), live_of(new_c)

    _, accs, _, _ = lax.while_loop(cond, body, (jnp.int32(0), accs, csums, live_of(csums)))
    for h in range(SB_HP):
        o_ref[:, heads[h]] = accs[h].astype(o_ref.dtype)


def _sb_attention(proj_d, batch, seq):
    nq = seq // SB_QB
    width = SB_PAST + SB_QB
    hw = SB_HP * LANES
    return pl.pallas_call(
        _sb_kernel,
        grid=(batch, SB_HEADS // SB_HP, nq),
        in_specs=[pl.BlockSpec((SB_QB, hw), lambda b, h, i: (b * nq + i, CB_SQ // SB_HP + h)),
                  pl.BlockSpec((seq, hw), lambda b, h, i: (b, CB_SK // SB_HP + h)),
                  pl.BlockSpec((seq, hw), lambda b, h, i: (b, CB_SV // SB_HP + h)),
                  pl.BlockSpec((width, width), lambda b, h, i: (0, 0))],
        out_specs=pl.BlockSpec((SB_QB, hw), lambda b, h, i: (b * nq + i, h)),
        out_shape=jax.ShapeDtypeStruct((batch * seq, SB_HEADS * LANES), BF16),
        compiler_params=_params(("parallel", "parallel", "arbitrary")),
        name="stick_breaking_attention",
    )(proj_d, proj_d, proj_d, _sb_suffix(width))


OUT_PIECES = (NSA_HEADS * LANES,) + (DIL_HEADS_PER_PAIR * LANES,) * 3 + (SB_HEADS * LANES,)


def _out_proj_kernel(a0, a1, a2, a3, a4, w_ref, x_ref, o_ref):
    acc = x_ref[...]
    off = 0
    for a_ref, width in zip((a0, a1, a2, a3, a4), OUT_PIECES):
        acc = acc + _dot(a_ref[...], w_ref[0, off:off + width, :].astype(BF16))
        off += width
    o_ref[...] = acc


def _out_proj(pieces, w3d, layer, x2d):
    m, n = x2d.shape
    tm, tn = 1024, 512
    a_specs = [pl.BlockSpec((tm, width), lambda i, j: (i, 0)) for width in OUT_PIECES]
    return pl.pallas_call(
        _out_proj_kernel,
        grid=(m // tm, n // tn),
        in_specs=a_specs + [pl.BlockSpec((1, w3d.shape[1], tn), lambda i, j: (layer, 0, j)),
                            pl.BlockSpec((tm, tn), lambda i, j: (i, j))],
        out_specs=pl.BlockSpec((tm, tn), lambda i, j: (i, j)),
        out_shape=jax.ShapeDtypeStruct((m, n), F32),
        compiler_params=_params(("parallel", "arbitrary")),
        name="out_proj",
    )(*pieces, w3d, x2d)


def _mlp_up_kernel(a_ref, w_ref, o_ref):
    r = jnp.maximum(_dot(a_ref[...], w_ref[0].astype(BF16)), 0.0)
    o_ref[...] = (r * r).astype(o_ref.dtype)


def _mlp_up(h, w3d, layer):
    m, k = h.shape
    n = w3d.shape[2]
    tm, tn = 2048, 256
    return pl.pallas_call(
        _mlp_up_kernel,
        grid=(m // tm, n // tn),
        in_specs=[pl.BlockSpec((tm, k), lambda i, j: (i, 0)),
                  pl.BlockSpec((1, k, tn), lambda i, j: (layer, 0, j))],
        out_specs=pl.BlockSpec((tm, tn), lambda i, j: (i, j)),
        out_shape=jax.ShapeDtypeStruct((m, n), BF16),
        compiler_params=_params(("parallel", "arbitrary")),
        name="mlp_up",
    )(h, w3d)


def _mlp_down_kernel(a_ref, w_ref, x_ref, o_ref):
    @pl.when(pl.program_id(2) == 0)
    def _():
        o_ref[...] = x_ref[...]

    o_ref[...] += _dot(a_ref[...], w_ref[0])


def _mlp_down(a, w3d, layer, x2d):
    m, kdim = a.shape
    n = w3d.shape[2]
    tm, tn, tk = 1024, 1024, 2048
    return pl.pallas_call(
        _mlp_down_kernel,
        grid=(m // tm, n // tn, kdim // tk),
        in_specs=[pl.BlockSpec((tm, tk), lambda i, j, k: (i, k)),
                  pl.BlockSpec((1, tk, tn), lambda i, j, k: (layer, k, j)),
                  pl.BlockSpec((tm, tn), lambda i, j, k: (i, j))],
        out_specs=pl.BlockSpec((tm, tn), lambda i, j, k: (i, j)),
        out_shape=jax.ShapeDtypeStruct((m, n), F32),
        compiler_params=_params(("parallel", "parallel", "arbitrary")),
        name="mlp_down",
    )(a, w3d, x2d)


def _rope_tables(seq):
    inv = 1.0 / (ROPE_THETA ** (jnp.arange(0, HEAD_DIM, 2, dtype=F32) / HEAD_DIM))
    ang = jnp.arange(seq, dtype=F32)[:, None] * inv[None, :]
    cos, sin = jnp.cos(ang), jnp.sin(ang)
    return jnp.concatenate([cos, cos], axis=-1), jnp.concatenate([-sin, sin], axis=-1)


def _nsa_mixer(proj_n, proj_b, batch, seq, pe_k, pe_v, w_ck1, w_ck2, w_cv1, w_cv2):
    nb = seq // NSA_BLOCK
    groups = NSA_KV_HEADS
    x_kv = proj_b[:B_GATE].reshape(2 * groups * batch * nb, NSA_BLOCK * HEAD_DIM)
    pe = jnp.stack([pe_k.reshape(1, -1), pe_v.reshape(1, -1)])
    w1 = jnp.stack([w_ck1, w_cv1]).astype(BF16)
    w2 = jnp.stack([w_ck2, w_cv2]).astype(BF16)
    cmp_kv = _compress(x_kv, pe, w1, w2)
    cmp_kv = cmp_kv.reshape(2, groups * batch, nb, HEAD_DIM)
    cmp_kv = jnp.pad(cmp_kv, ((0, 0), (0, 0), (0, NB_PAD - nb), (0, 0))).astype(BF16)
    kc = cmp_kv[0]
    vct = cmp_kv[1].transpose(0, 2, 1)
    return _nsa_attention(proj_n, kc, vct, proj_b, batch, seq)


def _layer(x2d, batch, seq, cosf, sinf, layer, w_in, w_tail, g_mix, w_out, g_mlp, w_up, w_down,
           pe_k, pe_v, w_ck1, w_ck2, w_cv1, w_cv2):
    h = _rmsnorm(x2d, g_mix, BF16)
    proj_n = _proj(h, w_in, layer, lambda j: jnp.where(j >= 3, j + 1, j), N_COLS // PROJ_TN, 3, 2,
                   ((0, 2, SCALE_LOG2E),), cosf, sinf, seq)
    q_tiles = DIL_HEADS * LANES // PROJ_TN
    sq_tile = CB_SQ * LANES // PROJ_TN
    proj_d = _proj(h, w_tail, layer, lambda j: j, A_COLS // PROJ_TN, 2 * q_tiles, 0,
                   ((0, q_tiles - 1, SCALE_LOG2E), (sq_tile, sq_tile + SB_HEADS * LANES // PROJ_TN - 1, SCALE)),
                   cosf, sinf, seq)
    proj_b = _proj_b(h, w_in, layer, cosf, sinf, seq)

    o_a = _nsa_mixer(proj_n, proj_b, batch, seq, pe_k, pe_v, w_ck1, w_ck2, w_cv1, w_cv2)
    o_b = _dil_attention(proj_d, batch, seq)
    o_c = _sb_attention(proj_d, batch, seq)

    x2d = _out_proj([o_a, *o_b, o_c], w_out, layer, x2d)
    h2 = _rmsnorm(x2d, g_mlp, BF16)
    up = _mlp_up(h2, w_up, layer)
    return _mlp_down(up, w_down, layer, x2d)


def kernel(x, w_in, w_out, norm_mix, norm_mlp, w_up, w_down, nsa_pe_k, nsa_pe_v,
           nsa_w_ck1, nsa_w_ck2, nsa_w_cv1, nsa_w_cv2, final_norm):
    batch, seq, d = x.shape
    cosf, sinf = _rope_tables(seq)
    depth = w_in.shape[0]
    w_head = jnp.transpose(w_in[:, :, :W_GATE + LANES], (0, 2, 1)).astype(BF16)
    w_tail = jnp.transpose(w_in[:, :, W_TAIL:], (0, 2, 1)).astype(BF16)
    w_down = w_down.astype(BF16)
    x2d = x.reshape(batch * seq, d)
    for layer in range(depth):
        x2d = _layer(x2d, batch, seq, cosf, sinf, layer, w_head, w_tail, norm_mix[layer], w_out,
                     norm_mlp[layer], w_up, w_down, nsa_pe_k[layer], nsa_pe_v[layer],
                     nsa_w_ck1[layer], nsa_w_ck2[layer], nsa_w_cv1[layer], nsa_w_cv2[layer])
    return _rmsnorm(x2d, final_norm, x.dtype).reshape(batch, seq, d)
```

```python
import functools

import jax
import jax.numpy as jnp
from jax import lax
from jax.experimental import pallas as pl
from jax.experimental.pallas import tpu as pltpu

F32 = jnp.float32
BF16 = jnp.bfloat16

HEAD_DIM = 128
ROPE_THETA = 10000.0
EPS = 1e-6
NEG = -1e30
NSA_HEADS = 12
NSA_KV_HEADS = 2
NSA_GROUP = NSA_HEADS // NSA_KV_HEADS
NSA_BLOCK = 64
NSA_TOPN = 16
NSA_WINDOW = 512
NSA_FORCED_SCORE = NSA_GROUP + 1.0
DIL_PAIRS = ((128, 1), (512, 4), (2048, 16))
DIL_HEADS_PER_PAIR = 4
DIL_HEADS = DIL_HEADS_PER_PAIR * len(DIL_PAIRS)
SB_HEADS = 8
SCALE_LOG2E = HEAD_DIM ** -0.5 * 1.4426950408889634

LANES = 128
VMEM_LIMIT = 56 * 1024 * 1024

KV_W = NSA_KV_HEADS * HEAD_DIM
W_KCMP = NSA_HEADS * HEAD_DIM
W_GATE = W_KCMP + 6 * KV_W
N_GATE = NSA_HEADS * 3
W_TAIL = W_GATE + N_GATE
PROJ_TN = 512
N_COLS = 20 * LANES
CB_NSA_Q, CB_KSLC, CB_VSLC, CB_KWIN, CB_VWIN = 0, 12, 14, 16, 18
A_COLS = 60 * LANES
CB_DQ, CB_DK, CB_DV = 0, 12, 24
CB_SQ, CB_SK, CB_SV = 36, 44, 52
B_CHUNKS = 5
B_GATE = 4


def _params(sem):
    return pltpu.CompilerParams(dimension_semantics=sem, vmem_limit_bytes=VMEM_LIMIT)


def _nt_dot(a, b):
    return lax.dot_general(a, b, (((1,), (1,)), ((), ())), preferred_element_type=F32)


def _dot(a, b):
    return jnp.dot(a, b, preferred_element_type=F32)


def _rmsnorm_kernel(x_ref, g_ref, o_ref):
    x = x_ref[...]
    ms = jnp.mean(x * x, axis=-1, keepdims=True)
    o_ref[...] = (x * lax.rsqrt(ms + EPS) * g_ref[...]).astype(o_ref.dtype)


def _rmsnorm(x2d, g, out_dtype):
    m, d = x2d.shape
    tm = 512
    return pl.pallas_call(
        _rmsnorm_kernel,
        grid=(m // tm,),
        in_specs=[pl.BlockSpec((tm, d), lambda i: (i, 0)),
                  pl.BlockSpec((1, d), lambda i: (0, 0))],
        out_specs=pl.BlockSpec((tm, d), lambda i: (i, 0)),
        out_shape=jax.ShapeDtypeStruct((m, d), out_dtype),
        compiler_params=_params(("parallel",)),
        name="rmsnorm",
    )(x2d, g.reshape(1, d))


def _rope(xc, cos, sin):
    return xc * cos + pltpu.roll(xc, HEAD_DIM // 2, 1) * sin


def _proj_kernel(a_ref, w_ref, cos_ref, sin_ref, o_ref, *, n_full, n_half, q_scales):
    j = pl.program_id(1)
    acc = _nt_dot(a_ref[...], w_ref[0].astype(BF16))
    factor = jnp.float32(1.0)
    for lo, hi, f in q_scales:
        factor = jnp.where((j >= lo) & (j <= hi), jnp.float32(f), factor)

    n_roped = jnp.where(j < n_full, PROJ_TN // LANES, jnp.where(j < n_full + n_half, NSA_KV_HEADS, 0))
    cos_f = cos_ref[...] * factor
    sin_f = sin_ref[...] * factor
    for c in range(PROJ_TN // LANES):
        sl = slice(c * LANES, (c + 1) * LANES)
        roped = n_roped > c
        o_ref[:, sl] = _rope(acc[:, sl], jnp.where(roped, cos_f, factor),
                             jnp.where(roped, sin_f, 0.0)).astype(o_ref.dtype)


def _proj(h, w3d, layer, src_tile, n_tiles, n_full, n_half, q_scales, cosf, sinf, seq):
    m, k = h.shape
    tm = 1024
    t_blocks = seq // tm
    return pl.pallas_call(
        functools.partial(_proj_kernel, n_full=n_full, n_half=n_half, q_scales=q_scales),
        grid=(m // tm, n_tiles),
        in_specs=[pl.BlockSpec((tm, k), lambda i, j: (i, 0)),
                  pl.BlockSpec((1, PROJ_TN, k), lambda i, j: (layer, src_tile(j), 0)),
                  pl.BlockSpec((tm, LANES), lambda i, j: (i % t_blocks, 0)),
                  pl.BlockSpec((tm, LANES), lambda i, j: (i % t_blocks, 0))],
        out_specs=pl.BlockSpec((tm, PROJ_TN), lambda i, j: (i, j)),
        out_shape=jax.ShapeDtypeStruct((m, n_tiles * PROJ_TN), BF16),
        compiler_params=_params(("parallel", "arbitrary")),
        name="proj",
    )(h, w3d, cosf, sinf)


def _proj_b_kernel(a_ref, w_ref, wg_ref, cos_ref, sin_ref, o_ref):
    a = a_ref[...]
    acc = _nt_dot(a, w_ref[0].astype(BF16))
    for c in range(B_GATE):
        xc = acc[:, c * LANES:(c + 1) * LANES]
        o_ref[c] = _rope(xc, cos_ref[...], sin_ref[...]) if c < NSA_KV_HEADS else xc
    o_ref[B_GATE] = _nt_dot(a, wg_ref[0].astype(BF16))


def _proj_b(h, w3d, layer, cosf, sinf, seq):
    m, k = h.shape
    tm = 1024
    t_blocks = seq // tm
    return pl.pallas_call(
        _proj_b_kernel,
        grid=(m // tm,),
        in_specs=[pl.BlockSpec((tm, k), lambda i: (i, 0)),
                  pl.BlockSpec((1, 2 * KV_W, k), lambda i: (layer, W_KCMP // (2 * KV_W), 0)),
                  pl.BlockSpec((1, LANES, k), lambda i: (layer, W_GATE // LANES, 0)),
                  pl.BlockSpec((tm, LANES), lambda i: (i % t_blocks, 0)),
                  pl.BlockSpec((tm, LANES), lambda i: (i % t_blocks, 0))],
        out_specs=pl.BlockSpec((B_CHUNKS, tm, LANES), lambda i: (0, i, 0)),
        out_shape=jax.ShapeDtypeStruct((B_CHUNKS, m, LANES), F32),
        compiler_params=_params(("parallel",)),
        name="proj_b",
    )(h, w3d, w3d, cosf, sinf)


def _compress_kernel(x_ref, pe_ref, w1_ref, w2_ref, o_ref):
    blk = (x_ref[...] + pe_ref[0]).astype(BF16)
    hid = _dot(blk, w1_ref[0])
    hid = hid * jax.nn.sigmoid(hid)
    o_ref[0] = _dot(hid.astype(BF16), w2_ref[0])


def _compress(x_kv, pe, w1, w2):
    rows = x_kv.shape[0] // 2
    kdim = x_kv.shape[1]
    hidden = w1.shape[-1]
    return pl.pallas_call(
        _compress_kernel,
        grid=(2,),
        in_specs=[pl.BlockSpec((rows, kdim), lambda i: (i, 0)),
                  pl.BlockSpec((1, 1, kdim), lambda i: (i, 0, 0)),
                  pl.BlockSpec((1, kdim, hidden), lambda i: (i, 0, 0)),
                  pl.BlockSpec((1, hidden, HEAD_DIM), lambda i: (i, 0, 0))],
        out_specs=pl.BlockSpec((1, rows, HEAD_DIM), lambda i: (i, 0, 0)),
        out_shape=jax.ShapeDtypeStruct((2, rows, HEAD_DIM), F32),
        compiler_params=_params(("parallel",)),
        name="nsa_compress",
    )(x_kv, pe, w1, w2)


NSA_QB = 256
NSA_KC = 128
NSA_KS = 256
NSA_CW = 512
NSA_WIN_KEYS = NSA_WINDOW + NSA_QB
NB_PAD = 128


def _nsa_kernel(q_ref, ks_ref, kw_ref, vs_ref, vw_ref, kc_ref, vct_ref, gate_ref, o_ref,
                vst_ref, vwt_ref, val_ref, sel_ref, gt_ref, acc_ref, base_ref, *, seq):
    i = pl.program_id(2)
    qb = NSA_QB
    n_blocks = seq // NSA_BLOCK
    q0 = pl.multiple_of(i * qb, qb)

    @pl.when(i == 0)
    def _():
        for c in range(seq // NSA_KC):
            rows = slice(c * NSA_KC, (c + 1) * NSA_KC)
            vwt_ref[c] = vw_ref[rows, :].astype(F32).T.astype(BF16)
        for c in range(seq // NSA_KS):
            rows = slice(c * NSA_KS, (c + 1) * NSA_KS)
            vst_ref[c] = vs_ref[rows, :].astype(F32).T.astype(BF16)

    qs = [q_ref[:, r * LANES:(r + 1) * LANES] for r in range(NSA_GROUP)]
    t_row = i * qb + lax.broadcasted_iota(jnp.int32, (1, qb), 1)
    sub_n = lax.broadcasted_iota(jnp.int32, (NB_PAD, qb), 0)
    t_full = i * qb + lax.broadcasted_iota(jnp.int32, (NB_PAD, qb), 1)

    cmp_ok = (sub_n * NSA_BLOCK + (NSA_BLOCK - 1) <= t_full) & (sub_n < n_blocks)
    any_ok = jnp.where(t_row >= NSA_BLOCK - 1, 1.0, 0.0)
    kc = kc_ref[0]
    vct = vct_ref[0]
    imp = jnp.zeros((NB_PAD, qb), F32)
    o_cmp = []
    for r in range(NSA_GROUP):
        s = jnp.where(cmp_ok, _nt_dot(kc, qs[r]), NEG)
        e = jnp.exp2(s - jnp.max(s, axis=0, keepdims=True))
        p = e * (any_ok / jnp.sum(e, axis=0, keepdims=True))
        imp = imp + p
        o_cmp.append(_dot(vct, p.astype(BF16)))

    cur = jnp.right_shift(t_full, NSA_BLOCK.bit_length() - 1)
    started = sub_n <= cur
    forced = (sub_n == 0) | (sub_n == cur) | (sub_n == cur - 1)
    val = jnp.where(forced, NSA_FORCED_SCORE, jnp.where(started, imp, -1.0))
    val_ref[...] = val
    val_n = val[:n_blocks]
    sub_nn = lax.broadcasted_iota(jnp.int32, (n_blocks, qb), 0)
    rank = jnp.zeros((n_blocks, qb), F32)
    for n2 in range(n_blocks):
        row = val_ref[n2:n2 + 1, :]
        rank = rank + jnp.where(sub_nn > n2, jnp.where(row >= val_n, 1.0, 0.0),
                                jnp.where(row > val_n, 1.0, 0.0))
    sel_ref[0:n_blocks, :] = jnp.where(rank < NSA_TOPN, 1.0, 0.0)

    per_step = NSA_CW // NSA_BLOCK
    d_cw = (lax.broadcasted_iota(jnp.int32, (NSA_CW, qb), 0)
            - lax.broadcasted_iota(jnp.int32, (NSA_CW, qb), 1))

    def slc_scores(cw):
        k0 = cw * NSA_CW if isinstance(cw, int) else pl.multiple_of(cw * NSA_CW, NSA_CW)
        k_chunk = ks_ref[pl.ds(k0, NSA_CW), :]
        picked = jnp.concatenate(
            [jnp.broadcast_to(sel_ref[pl.ds(per_step * cw + b, 1), :], (NSA_BLOCK, qb))
             for b in range(per_step)], axis=0)
        mask = jnp.where(d_cw <= q0 - k0, picked, 0.0) > 0.5
        return [jnp.where(mask, _nt_dot(k_chunk, qs[r]), NEG) for r in range(NSA_GROUP)]

    def pv(vt_ref, first_chunk, p, n_chunks, kc):
        out = None
        for j in range(n_chunks):
            part = _dot(vt_ref[first_chunk + j], p[j * kc:(j + 1) * kc].astype(BF16))
            out = part if out is None else out + part
        return out

    ms, ls = [], []
    for r, s in enumerate(slc_scores(0)):
        m = jnp.max(s, axis=0, keepdims=True)
        p = jnp.exp2(s - m)
        ms.append(m)
        ls.append(jnp.sum(p, axis=0, keepdims=True))
        acc_ref[r] = pv(vst_ref, 0, p, NSA_CW // NSA_KS, NSA_KS)

    w_start = pl.multiple_of(jnp.maximum(q0 - NSA_WINDOW, 0), NSA_KC)
    delta = ((q0 - w_start) + lax.broadcasted_iota(jnp.int32, (NSA_WIN_KEYS, qb), 1)
             - lax.broadcasted_iota(jnp.int32, (NSA_WIN_KEYS, qb), 0))
    in_win = (delta >= 0) & (delta < NSA_WINDOW)
    k_win = kw_ref[pl.ds(w_start, NSA_WIN_KEYS), :]
    first_chunk = jnp.right_shift(w_start, NSA_KC.bit_length() - 1)
    gt_ref[...] = jax.nn.sigmoid(gate_ref[0].T)
    g_row = pl.program_id(1) * (NSA_GROUP * 3)
    for r in range(NSA_GROUP):
        s = jnp.where(in_win, _nt_dot(k_win, qs[r]), NEG)
        p = jnp.exp2(s - jnp.max(s, axis=0, keepdims=True))
        l_win = jnp.sum(p, axis=0, keepdims=True)
        o_win = pv(vwt_ref, first_chunk, p, NSA_WIN_KEYS // NSA_KC, NSA_KC)
        g_cmp = gt_ref[pl.ds(g_row + 3 * r, 1), :]
        g_win = gt_ref[pl.ds(g_row + 3 * r + 2, 1), :]
        base_ref[r] = g_cmp * o_cmp[r] + (g_win / l_win) * o_win

    def slc_body(cw, carry):
        ms, ls = carry
        new_m, new_l = [], []
        for r, s in enumerate(slc_scores(cw)):
            m_new = jnp.maximum(ms[r], jnp.max(s, axis=0, keepdims=True))
            alpha = jnp.exp2(ms[r] - m_new)
            p = jnp.exp2(s - m_new)
            new_m.append(m_new)
            new_l.append(alpha * ls[r] + jnp.sum(p, axis=0, keepdims=True))
            acc_ref[r] = alpha * acc_ref[r] + pv(vst_ref, cw * (NSA_CW // NSA_KS), p,
                                                 NSA_CW // NSA_KS, NSA_KS)
        return tuple(new_m), tuple(new_l)

    n_steps = jnp.right_shift(q0 + (qb + NSA_CW - 1), NSA_CW.bit_length() - 1)
    _, ls = lax.fori_loop(1, n_steps, slc_body, (tuple(ms), tuple(ls)))

    for r in range(NSA_GROUP):
        g_slc = gt_ref[pl.ds(g_row + 3 * r + 1, 1), :]
        o = base_ref[r] + (g_slc / ls[r]) * acc_ref[r]
        o_ref[:, r * LANES:(r + 1) * LANES] = o.T.astype(o_ref.dtype)


def _nsa_attention(proj_n, kc, vct, proj_b, batch, seq):
    qb = NSA_QB
    nq = seq // qb
    n_chunks = seq // NSA_KC
    gw = NSA_GROUP * LANES
    kv_spec = lambda base: pl.BlockSpec((seq, LANES), lambda b, g, i: (b, base + g))
    return pl.pallas_call(
        functools.partial(_nsa_kernel, seq=seq),
        grid=(batch, NSA_KV_HEADS, nq),
        in_specs=[
            pl.BlockSpec((qb, gw), lambda b, g, i: (b * nq + i, g)),
            kv_spec(CB_KSLC), kv_spec(CB_KWIN), kv_spec(CB_VSLC), kv_spec(CB_VWIN),
            pl.BlockSpec((1, NB_PAD, LANES), lambda b, g, i: (g * batch + b, 0, 0)),
            pl.BlockSpec((1, LANES, NB_PAD), lambda b, g, i: (g * batch + b, 0, 0)),
            pl.BlockSpec((1, qb, LANES), lambda b, g, i: (B_GATE, b * nq + i, 0)),
        ],
        out_specs=pl.BlockSpec((qb, gw), lambda b, g, i: (b * nq + i, g)),
        out_shape=jax.ShapeDtypeStruct((batch * seq, NSA_HEADS * LANES), BF16),
        scratch_shapes=[
            pltpu.VMEM((seq // NSA_KS, LANES, NSA_KS), BF16),
            pltpu.VMEM((n_chunks, LANES, NSA_KC), BF16),
            pltpu.VMEM((NB_PAD, qb), F32),
            pltpu.VMEM((NB_PAD, qb), F32),
            pltpu.VMEM((LANES, qb), F32),
            pltpu.VMEM((NSA_GROUP, LANES, qb), F32),
            pltpu.VMEM((NSA_GROUP, LANES, qb), F32),
        ],
        compiler_params=_params(("arbitrary", "arbitrary", "arbitrary")),
        name="nsa_attention",
    )(proj_n, proj_n, proj_n, proj_n, proj_n, kc, vct, proj_b)


DIL_QB = 256


def _dil_bias(window, dil):
    rows = jnp.arange(DIL_QB, dtype=jnp.int32)[:, None]
    cols = jnp.arange(window + DIL_QB, dtype=jnp.int32)[None, :]
    delta = window + rows - cols
    ok = (delta >= 0) & (delta <= window) & (delta % dil == 0)
    return jnp.where(ok, 0.0, NEG).astype(F32)


def _dil_kernel(q0_ref, q1_ref, q2_ref, k0_ref, k1_ref, k2_ref, v0_ref, v1_ref, v2_ref,
                b0_ref, b1_ref, b2_ref, o0_ref, o1_ref, o2_ref,
                kp0_ref, kp1_ref, kp2_ref, vp0_ref, vp1_ref, vp2_ref, *, seq):
    i = pl.program_id(2)
    qs = (q0_ref, q1_ref, q2_ref)
    ks = (k0_ref, k1_ref, k2_ref)
    vs = (v0_ref, v1_ref, v2_ref)
    biases = (b0_ref, b1_ref, b2_ref)
    kps = (kp0_ref, kp1_ref, kp2_ref)
    vps = (vp0_ref, vp1_ref, vp2_ref)
    outs = (o0_ref, o1_ref, o2_ref)

    @pl.when(i == 0)
    def _():
        for p, (window, _) in enumerate(DIL_PAIRS):
            kps[p][0:window, :] = jnp.zeros((window, LANES), BF16)
            vps[p][0:window, :] = jnp.zeros((window, LANES), BF16)
            kps[p][window:window + seq, :] = ks[p][...]
            vps[p][window:window + seq, :] = vs[p][...]

    q_start = pl.multiple_of(i * DIL_QB, DIL_QB)
    accs, lses = [], []
    for p, (window, _) in enumerate(DIL_PAIRS):
        width = window + DIL_QB
        k_win = kps[p][pl.ds(q_start, width), :]
        v_win = vps[p][pl.ds(q_start, width), :]
        col = lax.broadcasted_iota(jnp.int32, (1, width), 1)
        in_seq = jnp.where(col >= window - q_start, 0.0, NEG)
        s = _nt_dot(qs[p][...], k_win) + biases[p][...] + in_seq
        m = jnp.max(s, axis=1, keepdims=True)
        e = jnp.exp2(s - m)
        l = jnp.sum(e, axis=1, keepdims=True)
        accs.append(_dot(e.astype(BF16), v_win) / l)
        lses.append(m + jnp.log2(l))

    top = jnp.maximum(jnp.maximum(lses[0], lses[1]), lses[2])
    ws = [jnp.exp2(x - top) for x in lses]
    inv = 1.0 / (ws[0] + ws[1] + ws[2])
    for p in range(len(DIL_PAIRS)):
        outs[p][...] = (accs[p] * (ws[p] * inv)).astype(outs[p].dtype)


def _dil_attention(proj_d, batch, seq):
    nq = seq // DIL_QB
    hp = DIL_HEADS_PER_PAIR

    def q_spec(p):
        return pl.BlockSpec((DIL_QB, LANES), lambda b, h, i: (b * nq + i, CB_DQ + p * hp + h))

    def kv_spec(base, p):
        return pl.BlockSpec((seq, LANES), lambda b, h, i: (b, base + p * hp + h))

    def bias_spec(p):
        return pl.BlockSpec((DIL_QB, DIL_PAIRS[p][0] + DIL_QB), lambda b, h, i: (0, 0))

    out_spec = pl.BlockSpec((DIL_QB, LANES), lambda b, h, i: (b * nq + i, h))
    out_shape = jax.ShapeDtypeStruct((batch * seq, hp * LANES), BF16)
    staged = [pltpu.VMEM((w + seq, LANES), BF16) for w, _ in DIL_PAIRS]
    return pl.pallas_call(
        functools.partial(_dil_kernel, seq=seq),
        grid=(batch, hp, nq),
        in_specs=[q_spec(0), q_spec(1), q_spec(2),
                  kv_spec(CB_DK, 0), kv_spec(CB_DK, 1), kv_spec(CB_DK, 2),
                  kv_spec(CB_DV, 0), kv_spec(CB_DV, 1), kv_spec(CB_DV, 2),
                  bias_spec(0), bias_spec(1), bias_spec(2)],
        out_specs=[out_spec, out_spec, out_spec],
        out_shape=[out_shape, out_shape, out_shape],
        scratch_shapes=staged + staged,
        compiler_params=_params(("arbitrary", "arbitrary", "arbitrary")),
        name="dilated_attention",
    )(*([proj_d] * 9), *[_dil_bias(w, d) for w, d in DIL_PAIRS])


SB_QB = 256
SB_PAST = 256
SB_KB = 128
SB_DONE = -151.0


def _sb_suffix(n):
    j = jnp.arange(n, dtype=jnp.int32)[:, None]
    s = jnp.arange(n, dtype=jnp.int32)[None, :]
    return jnp.where(j > s, 1.0, 0.0).astype(BF16)


SB_HP = 4


def _sb_kernel(q_ref, k_ref, v_ref, suffix_ref, o_ref):
    i = pl.program_id(2)
    width = SB_PAST + SB_QB

    def tile(q, k_chunk, v_chunk, suffix, carry_sum, before):
        z = _nt_dot(q, k_chunk)
        soft = jnp.log2(1.0 + jnp.exp2(-jnp.abs(z)))
        log_beta = jnp.minimum(z, 0.0) - soft
        log_1m = -jnp.maximum(z, 0.0) - soft
        if before is not None:
            log_1m = jnp.where(before, log_1m, 0.0)
        hi = log_1m.astype(BF16)
        lo = (log_1m - hi.astype(F32)).astype(BF16)
        logit = log_beta + (_dot(hi, suffix) + _dot(lo, suffix)) + carry_sum
        if before is not None:
            logit = jnp.where(before, logit, NEG)
        contrib = _dot(jnp.exp2(logit).astype(BF16), v_chunk)
        return contrib, carry_sum + jnp.sum(log_1m, axis=1, keepdims=True)

    q_start = i * SB_QB
    w_start = pl.multiple_of(jnp.maximum(q_start - SB_PAST, 0), SB_QB)
    offs = (lax.broadcasted_iota(jnp.int32, (SB_QB, width), 1)
            - lax.broadcasted_iota(jnp.int32, (SB_QB, width), 0))
    before = offs < q_start - w_start
    heads = [slice(h * LANES, (h + 1) * LANES) for h in range(SB_HP)]
    qs = [q_ref[:, sl] for sl in heads]
    state = [tile(qs[h], k_ref[pl.ds(w_start, width), heads[h]], v_ref[pl.ds(w_start, width), heads[h]],
                  suffix_ref[...], jnp.zeros((SB_QB, 1), F32), before) for h in range(SB_HP)]
    accs = tuple(s[0] for s in state)
    csums = tuple(s[1] for s in state)

    n_rest = jnp.right_shift(w_start, SB_KB.bit_length() - 1)
    suffix_kb = suffix_ref[0:SB_KB, 0:SB_KB]

    def live_of(csums):
        top = csums[0]
        for c in csums[1:]:
            top = jnp.maximum(top, c)
        return jnp.max(top)

    def cond(carry):
        c, _, _, live = carry
        return (c < n_rest) & (live > SB_DONE)

    def body(carry):
        c, accs, csums, _ = carry
        k_start = pl.multiple_of((n_rest - 1 - c) * SB_KB, SB_KB)
        new_a, new_c = [], []
        for h in range(SB_HP):
            contrib, cs = tile(qs[h], k_ref[pl.ds(k_start, SB_KB), heads[h]],
                               v_ref[pl.ds(k_start, SB_KB), heads[h]], suffix_kb, csums[h], None)
            new_a.append(accs[h] + contrib)
            new_c.append(cs)
        return c + 1, tuple(new_a), tuple(new_c), live_of(new_c)

    _, accs, _, _ = lax.while_loop(cond, body, (jnp.int32(0), accs, csums, live_of(csums)))
    for h in range(SB_HP):
        o_ref[:, heads[h]] = accs[h].astype(o_ref.dtype)


def _sb_attention(proj_d, batch, seq):
    nq = seq // SB_QB
    width = SB_PAST + SB_QB
    hw = SB_HP * LANES
    return pl.pallas_call(
        _sb_kernel,
        grid=(batch, SB_HEADS // SB_HP, nq),
        in_specs=[pl.BlockSpec((SB_QB, hw), lambda b, h, i: (b * nq + i, CB_SQ // SB_HP + h)),
                  pl.BlockSpec((seq, hw), lambda b, h, i: (b, CB_SK // SB_HP + h)),
                  pl.BlockSpec((seq, hw), lambda b, h, i: (b, CB_SV // SB_HP + h)),
                  pl.BlockSpec((width, width), lambda b, h, i: (0, 0))],
        out_specs=pl.BlockSpec((SB_QB, hw), lambda b, h, i: (b * nq + i, h)),
        out_shape=jax.ShapeDtypeStruct((batch * seq, SB_HEADS * LANES), BF16),
        compiler_params=_params(("parallel", "parallel", "arbitrary")),
        name="stick_breaking_attention",
    )(proj_d, proj_d, proj_d, _sb_suffix(width))


OUT_PIECES = (NSA_HEADS * LANES,) + (DIL_HEADS_PER_PAIR * LANES,) * 3 + (SB_HEADS * LANES,)


def _out_proj_kernel(a0, a1, a2, a3, a4, w_ref, x_ref, o_ref):
    acc = x_ref[...]
    off = 0
    for a_ref, width in zip((a0, a1, a2, a3, a4), OUT_PIECES):
        acc = acc + _dot(a_ref[...], w_ref[0, off:off + width, :].astype(BF16))
        off += width
    o_ref[...] = acc


def _out_proj(pieces, w3d, layer, x2d):
    m, n = x2d.shape
    tm, tn = 1024, 512
    a_specs = [pl.BlockSpec((tm, width), lambda i, j: (i, 0)) for width in OUT_PIECES]
    return pl.pallas_call(
        _out_proj_kernel,
        grid=(m // tm, n // tn),
        in_specs=a_specs + [pl.BlockSpec((1, w3d.shape[1], tn), lambda i, j: (layer, 0, j)),
                            pl.BlockSpec((tm, tn), lambda i, j: (i, j))],
        out_specs=pl.BlockSpec((tm, tn), lambda i, j: (i, j)),
        out_shape=jax.ShapeDtypeStruct((m, n), F32),
        compiler_params=_params(("parallel", "arbitrary")),
        name="out_proj",
    )(*pieces, w3d, x2d)


def _mlp_up_kernel(a_ref, w_ref, o_ref):
    r = jnp.maximum(_dot(a_ref[...], w_ref[0].astype(BF16)), 0.0)
    o_ref[...] = (r * r).astype(o_ref.dtype)


def _mlp_up(h, w3d, layer):
    m, k = h.shape
    n = w3d.shape[2]
    tm, tn = 1024, 512
    return pl.pallas_call(
        _mlp_up_kernel,
        grid=(m // tm, n // tn),
        in_specs=[pl.BlockSpec((tm, k), lambda i, j: (i, 0)),
                  pl.BlockSpec((1, k, tn), lambda i, j: (layer, 0, j))],
        out_specs=pl.BlockSpec((tm, tn), lambda i, j: (i, j)),
        out_shape=jax.ShapeDtypeStruct((m, n), BF16),
        compiler_params=_params(("parallel", "arbitrary")),
        name="mlp_up",
    )(h, w3d)


def _mlp_down_kernel(a_ref, w_ref, x_ref, o_ref):
    @pl.when(pl.program_id(2) == 0)
    def _():
        o_ref[...] = x_ref[...]

    o_ref[...] += _dot(a_ref[...], w_ref[0])


def _mlp_down(a, w3d, layer, x2d):
    m, kdim = a.shape
    n = w3d.shape[2]
    tm, tn, tk = 1024, 1024, 2048
    return pl.pallas_call(
        _mlp_down_kernel,
        grid=(m // tm, n // tn, kdim // tk),
        in_specs=[pl.BlockSpec((tm, tk), lambda i, j, k: (i, k)),
                  pl.BlockSpec((1, tk, tn), lambda i, j, k: (layer, k, j)),
                  pl.BlockSpec((tm, tn), lambda i, j, k: (i, j))],
        out_specs=pl.BlockSpec((tm, tn), lambda i, j, k: (i, j)),
        out_shape=jax.ShapeDtypeStruct((m, n), F32),
        compiler_params=_params(("parallel", "parallel", "arbitrary")),
        name="mlp_down",
    )(a, w3d, x2d)


def _rope_tables(seq):
    inv = 1.0 / (ROPE_THETA ** (jnp.arange(0, HEAD_DIM, 2, dtype=F32) / HEAD_DIM))
    ang = jnp.arange(seq, dtype=F32)[:, None] * inv[None, :]
    cos, sin = jnp.cos(ang), jnp.sin(ang)
    return jnp.concatenate([cos, cos], axis=-1), jnp.concatenate([-sin, sin], axis=-1)


def _nsa_mixer(proj_n, proj_b, batch, seq, pe_k, pe_v, w_ck1, w_ck2, w_cv1, w_cv2):
    nb = seq // NSA_BLOCK
    groups = NSA_KV_HEADS
    x_kv = proj_b[:B_GATE].reshape(2 * groups * batch * nb, NSA_BLOCK * HEAD_DIM)
    pe = jnp.stack([pe_k.reshape(1, -1), pe_v.reshape(1, -1)])
    w1 = jnp.stack([w_ck1, w_cv1]).astype(BF16)
    w2 = jnp.stack([w_ck2, w_cv2]).astype(BF16)
    cmp_kv = _compress(x_kv, pe, w1, w2)
    cmp_kv = cmp_kv.reshape(2, groups * batch, nb, HEAD_DIM)
    cmp_kv = jnp.pad(cmp_kv, ((0, 0), (0, 0), (0, NB_PAD - nb), (0, 0))).astype(BF16)
    kc = cmp_kv[0]
    vct = cmp_kv[1].transpose(0, 2, 1)
    return _nsa_attention(proj_n, kc, vct, proj_b, batch, seq)


def _layer(x2d, batch, seq, cosf, sinf, layer, w_in, w_tail, g_mix, w_out, g_mlp, w_up, w_down,
           pe_k, pe_v, w_ck1, w_ck2, w_cv1, w_cv2):
    h = _rmsnorm(x2d, g_mix, BF16)
    proj_n = _proj(h, w_in, layer, lambda j: jnp.where(j >= 3, j + 1, j), N_COLS // PROJ_TN, 3, 2,
                   ((0, 2, SCALE_LOG2E),), cosf, sinf, seq)
    q_tiles = DIL_HEADS * LANES // PROJ_TN
    sq_tile = CB_SQ * LANES // PROJ_TN
    proj_d = _proj(h, w_tail, layer, lambda j: j, A_COLS // PROJ_TN, 2 * q_tiles, 0,
                   ((0, q_tiles - 1, SCALE_LOG2E),
                    (sq_tile, sq_tile + SB_HEADS * LANES // PROJ_TN - 1, SCALE_LOG2E)),
                   cosf, sinf, seq)
    proj_b = _proj_b(h, w_in, layer, cosf, sinf, seq)

    o_a = _nsa_mixer(proj_n, proj_b, batch, seq, pe_k, pe_v, w_ck1, w_ck2, w_cv1, w_cv2)
    o_b = _dil_attention(proj_d, batch, seq)
    o_c = _sb_attention(proj_d, batch, seq)

    x2d = _out_proj([o_a, *o_b, o_c], w_out, layer, x2d)
    h2 = _rmsnorm(x2d, g_mlp, BF16)
    up = _mlp_up(h2, w_up, layer)
    return _mlp_down(up, w_down, layer, x2d)


def kernel(x, w_in, w_out, norm_mix, norm_mlp, w_up, w_down, nsa_pe_k, nsa_pe_v,
           nsa_w_ck1, nsa_w_ck2, nsa_w_cv1, nsa_w_cv2, final_norm):
    batch, seq, d = x.shape
    cosf, sinf = _rope_tables(seq)
    depth = w_in.shape[0]
    w_head = jnp.transpose(w_in[:, :, :W_GATE + LANES], (0, 2, 1)).astype(BF16)
    w_tail = jnp.transpose(w_in[:, :, W_TAIL:], (0, 2, 1)).astype(BF16)
    w_down = w_down.astype(BF16)
    x2d = x.reshape(batch * seq, d)
    for layer in range(depth):
        x2d = _layer(x2d, batch, seq, cosf, sinf, layer, w_head, w_tail, norm_mix[layer], w_out,
                     norm_mlp[layer], w_up, w_down, nsa_pe_k[layer], nsa_pe_v[layer],
                     nsa_w_ck1[layer], nsa_w_ck2[layer], nsa_w_cv1[layer], nsa_w_cv2[layer])
    return _rmsnorm(x2d, final_norm, x.dtype).reshape(batch, seq, d)
```

```python
import functools

import jax
import jax.numpy as jnp
from jax import lax
from jax.experimental import pallas as pl
from jax.experimental.pallas import tpu as pltpu

F32 = jnp.float32
BF16 = jnp.bfloat16

HEAD_DIM = 128
ROPE_THETA = 10000.0
EPS = 1e-6
NEG = -1e30
NSA_HEADS = 12
NSA_KV_HEADS = 2
NSA_GROUP = NSA_HEADS // NSA_KV_HEADS
NSA_BLOCK = 64
NSA_TOPN = 16
NSA_WINDOW = 512
NSA_FORCED_SCORE = NSA_GROUP + 1.0
DIL_PAIRS = ((128, 1), (512, 4), (2048, 16))
DIL_HEADS_PER_PAIR = 4
DIL_HEADS = DIL_HEADS_PER_PAIR * len(DIL_PAIRS)
SB_HEADS = 8
SCALE_LOG2E = HEAD_DIM ** -0.5 * 1.4426950408889634

LANES = 128
VMEM_LIMIT = 56 * 1024 * 1024

KV_W = NSA_KV_HEADS * HEAD_DIM
W_KCMP = NSA_HEADS * HEAD_DIM
W_GATE = W_KCMP + 6 * KV_W
N_GATE = NSA_HEADS * 3
W_TAIL = W_GATE + N_GATE
PROJ_TN = 512
N_COLS = 20 * LANES
CB_NSA_Q, CB_KSLC, CB_VSLC, CB_KWIN, CB_VWIN = 0, 12, 14, 16, 18
A_COLS = 60 * LANES
CB_DQ, CB_DK, CB_DV = 0, 12, 24
CB_SQ, CB_SK, CB_SV = 36, 44, 52
B_CHUNKS = 5
B_GATE = 4


def _params(sem):
    return pltpu.CompilerParams(dimension_semantics=sem, vmem_limit_bytes=VMEM_LIMIT)


def _nt_dot(a, b):
    return lax.dot_general(a, b, (((1,), (1,)), ((), ())), preferred_element_type=F32)


def _dot(a, b):
    return jnp.dot(a, b, preferred_element_type=F32)


def _rmsnorm_kernel(x_ref, g_ref, o_ref):
    x = x_ref[...]
    ms = jnp.mean(x * x, axis=-1, keepdims=True)
    o_ref[...] = (x * lax.rsqrt(ms + EPS) * g_ref[...]).astype(o_ref.dtype)


def _rmsnorm(x2d, g, out_dtype):
    m, d = x2d.shape
    tm = 256
    return pl.pallas_call(
        _rmsnorm_kernel,
        grid=(m // tm,),
        in_specs=[pl.BlockSpec((tm, d), lambda i: (i, 0)),
                  pl.BlockSpec((1, d), lambda i: (0, 0))],
        out_specs=pl.BlockSpec((tm, d), lambda i: (i, 0)),
        out_shape=jax.ShapeDtypeStruct((m, d), out_dtype),
        compiler_params=_params(("parallel",)),
        name="rmsnorm",
    )(x2d, g.reshape(1, d))


def _rope(xc, cos, sin):
    return xc * cos + pltpu.roll(xc, HEAD_DIM // 2, 1) * sin


def _proj_kernel(a_ref, w_ref, cos_ref, sin_ref, o_ref, *, n_full, n_half, q_scales):
    j = pl.program_id(1)
    acc = _nt_dot(a_ref[...], w_ref[0].astype(BF16))
    factor = jnp.float32(1.0)
    for lo, hi, f in q_scales:
        factor = jnp.where((j >= lo) & (j <= hi), jnp.float32(f), factor)

    n_roped = jnp.where(j < n_full, PROJ_TN // LANES, jnp.where(j < n_full + n_half, NSA_KV_HEADS, 0))
    cos_f = cos_ref[...] * factor
    sin_f = sin_ref[...] * factor
    for c in range(PROJ_TN // LANES):
        sl = slice(c * LANES, (c + 1) * LANES)
        roped = n_roped > c
        o_ref[:, sl] = _rope(acc[:, sl], jnp.where(roped, cos_f, factor),
                             jnp.where(roped, sin_f, 0.0)).astype(o_ref.dtype)


def _proj(h, w3d, layer, src_tile, n_tiles, n_full, n_half, q_scales, cosf, sinf, seq):
    m, k = h.shape
    tm = 1024
    t_blocks = seq // tm
    return pl.pallas_call(
        functools.partial(_proj_kernel, n_full=n_full, n_half=n_half, q_scales=q_scales),
        grid=(m // tm, n_tiles),
        in_specs=[pl.BlockSpec((tm, k), lambda i, j: (i, 0)),
                  pl.BlockSpec((1, PROJ_TN, k), lambda i, j: (layer, src_tile(j), 0)),
                  pl.BlockSpec((tm, LANES), lambda i, j: (i % t_blocks, 0)),
                  pl.BlockSpec((tm, LANES), lambda i, j: (i % t_blocks, 0))],
        out_specs=pl.BlockSpec((tm, PROJ_TN), lambda i, j: (i, j)),
        out_shape=jax.ShapeDtypeStruct((m, n_tiles * PROJ_TN), BF16),
        compiler_params=_params(("parallel", "arbitrary")),
        name="proj",
    )(h, w3d, cosf, sinf)


def _proj_b_kernel(a_ref, w_ref, wg_ref, cos_ref, sin_ref, o_ref):
    a = a_ref[...]
    acc = _nt_dot(a, w_ref[0].astype(BF16))
    for c in range(B_GATE):
        xc = acc[:, c * LANES:(c + 1) * LANES]
        o_ref[c] = _rope(xc, cos_ref[...], sin_ref[...]) if c < NSA_KV_HEADS else xc
    o_ref[B_GATE] = _nt_dot(a, wg_ref[0].astype(BF16))


def _proj_b(h, w3d, layer, cosf, sinf, seq):
    m, k = h.shape
    tm = 1024
    t_blocks = seq // tm
    return pl.pallas_call(
        _proj_b_kernel,
        grid=(m // tm,),
        in_specs=[pl.BlockSpec((tm, k), lambda i: (i, 0)),
                  pl.BlockSpec((1, 2 * KV_W, k), lambda i: (layer, W_KCMP // (2 * KV_W), 0)),
                  pl.BlockSpec((1, LANES, k), lambda i: (layer, W_GATE // LANES, 0)),
                  pl.BlockSpec((tm, LANES), lambda i: (i % t_blocks, 0)),
                  pl.BlockSpec((tm, LANES), lambda i: (i % t_blocks, 0))],
        out_specs=pl.BlockSpec((B_CHUNKS, tm, LANES), lambda i: (0, i, 0)),
        out_shape=jax.ShapeDtypeStruct((B_CHUNKS, m, LANES), F32),
        compiler_params=_params(("parallel",)),
        name="proj_b",
    )(h, w3d, w3d, cosf, sinf)


def _compress_kernel(x_ref, pe_ref, w1_ref, w2_ref, o_ref):
    blk = (x_ref[...] + pe_ref[0]).astype(BF16)
    hid = _dot(blk, w1_ref[0])
    hid = hid * jax.nn.sigmoid(hid)
    o_ref[0] = _dot(hid.astype(BF16), w2_ref[0])


def _compress(x_kv, pe, w1, w2):
    rows = x_kv.shape[0] // 2
    kdim = x_kv.shape[1]
    hidden = w1.shape[-1]
    return pl.pallas_call(
        _compress_kernel,
        grid=(2,),
        in_specs=[pl.BlockSpec((rows, kdim), lambda i: (i, 0)),
                  pl.BlockSpec((1, 1, kdim), lambda i: (i, 0, 0)),
                  pl.BlockSpec((1, kdim, hidden), lambda i: (i, 0, 0)),
                  pl.BlockSpec((1, hidden, HEAD_DIM), lambda i: (i, 0, 0))],
        out_specs=pl.BlockSpec((1, rows, HEAD_DIM), lambda i: (i, 0, 0)),
        out_shape=jax.ShapeDtypeStruct((2, rows, HEAD_DIM), F32),
        compiler_params=_params(("parallel",)),
        name="nsa_compress",
    )(x_kv, pe, w1, w2)


NSA_QB = 256
NSA_KC = 128
NSA_KS = 256
NSA_CW = 512
NSA_WIN_KEYS = NSA_WINDOW + NSA_QB
NB_PAD = 128


def _nsa_kernel(q_ref, ks_ref, kw_ref, vs_ref, vw_ref, kc_ref, vct_ref, gate_ref, o_ref,
                vst_ref, vwt_ref, val_ref, sel_ref, gt_ref, acc_ref, base_ref, *, seq):
    i = pl.program_id(2)
    qb = NSA_QB
    n_blocks = seq // NSA_BLOCK
    q0 = pl.multiple_of(i * qb, qb)

    @pl.when(i == 0)
    def _():
        for c in range(seq // NSA_KC):
            rows = slice(c * NSA_KC, (c + 1) * NSA_KC)
            vwt_ref[c] = vw_ref[rows, :].astype(F32).T.astype(BF16)
        for c in range(seq // NSA_KS):
            rows = slice(c * NSA_KS, (c + 1) * NSA_KS)
            vst_ref[c] = vs_ref[rows, :].astype(F32).T.astype(BF16)

    qs = [q_ref[:, r * LANES:(r + 1) * LANES] for r in range(NSA_GROUP)]
    t_row = i * qb + lax.broadcasted_iota(jnp.int32, (1, qb), 1)
    sub_n = lax.broadcasted_iota(jnp.int32, (NB_PAD, qb), 0)
    t_full = i * qb + lax.broadcasted_iota(jnp.int32, (NB_PAD, qb), 1)

    cmp_ok = (sub_n * NSA_BLOCK + (NSA_BLOCK - 1) <= t_full) & (sub_n < n_blocks)
    any_ok = jnp.where(t_row >= NSA_BLOCK - 1, 1.0, 0.0)
    kc = kc_ref[0]
    vct = vct_ref[0]
    imp = jnp.zeros((NB_PAD, qb), F32)
    o_cmp = []
    for r in range(NSA_GROUP):
        s = jnp.where(cmp_ok, _nt_dot(kc, qs[r]), NEG)
        e = jnp.exp2(s - jnp.max(s, axis=0, keepdims=True))
        p = e * (any_ok / jnp.sum(e, axis=0, keepdims=True))
        imp = imp + p
        o_cmp.append(_dot(vct, p.astype(BF16)))

    cur = jnp.right_shift(t_full, NSA_BLOCK.bit_length() - 1)
    started = sub_n <= cur
    forced = (sub_n == 0) | (sub_n == cur) | (sub_n == cur - 1)
    val = jnp.where(forced, NSA_FORCED_SCORE, jnp.where(started, imp, -1.0))
    val_ref[...] = val
    val_n = val[:n_blocks]
    sub_nn = lax.broadcasted_iota(jnp.int32, (n_blocks, qb), 0)
    rank = jnp.zeros((n_blocks, qb), F32)
    for n2 in range(n_blocks):
        row = val_ref[n2:n2 + 1, :]
        rank = rank + jnp.where(sub_nn > n2, jnp.where(row >= val_n, 1.0, 0.0),
                                jnp.where(row > val_n, 1.0, 0.0))
    sel_ref[0:n_blocks, :] = jnp.where(rank < NSA_TOPN, 1.0, 0.0)

    per_step = NSA_CW // NSA_BLOCK
    d_cw = (lax.broadcasted_iota(jnp.int32, (NSA_CW, qb), 0)
            - lax.broadcasted_iota(jnp.int32, (NSA_CW, qb), 1))

    def slc_scores(cw):
        k0 = cw * NSA_CW if isinstance(cw, int) else pl.multiple_of(cw * NSA_CW, NSA_CW)
        k_chunk = ks_ref[pl.ds(k0, NSA_CW), :]
        picked = jnp.concatenate(
            [jnp.broadcast_to(sel_ref[pl.ds(per_step * cw + b, 1), :], (NSA_BLOCK, qb))
             for b in range(per_step)], axis=0)
        mask = jnp.where(d_cw <= q0 - k0, picked, 0.0) > 0.5
        return [jnp.where(mask, _nt_dot(k_chunk, qs[r]), NEG) for r in range(NSA_GROUP)]

    def pv(vt_ref, first_chunk, p, n_chunks, kc):
        out = None
        for j in range(n_chunks):
            part = _dot(vt_ref[first_chunk + j], p[j * kc:(j + 1) * kc].astype(BF16))
            out = part if out is None else out + part
        return out

    ms, ls = [], []
    for r, s in enumerate(slc_scores(0)):
        m = jnp.max(s, axis=0, keepdims=True)
        p = jnp.exp2(s - m)
        ms.append(m)
        ls.append(jnp.sum(p, axis=0, keepdims=True))
        acc_ref[r] = pv(vst_ref, 0, p, NSA_CW // NSA_KS, NSA_KS)

    w_start = pl.multiple_of(jnp.maximum(q0 - NSA_WINDOW, 0), NSA_KC)
    delta = ((q0 - w_start) + lax.broadcasted_iota(jnp.int32, (NSA_WIN_KEYS, qb), 1)
             - lax.broadcasted_iota(jnp.int32, (NSA_WIN_KEYS, qb), 0))
    in_win = (delta >= 0) & (delta < NSA_WINDOW)
    k_win = kw_ref[pl.ds(w_start, NSA_WIN_KEYS), :]
    first_chunk = jnp.right_shift(w_start, NSA_KC.bit_length() - 1)
    gt_ref[...] = jax.nn.sigmoid(gate_ref[0].T)
    g_row = pl.program_id(1) * (NSA_GROUP * 3)
    for r in range(NSA_GROUP):
        s = jnp.where(in_win, _nt_dot(k_win, qs[r]), NEG)
        p = jnp.exp2(s - jnp.max(s, axis=0, keepdims=True))
        l_win = jnp.sum(p, axis=0, keepdims=True)
        o_win = pv(vwt_ref, first_chunk, p, NSA_WIN_KEYS // NSA_KC, NSA_KC)
        g_cmp = gt_ref[pl.ds(g_row + 3 * r, 1), :]
        g_win = gt_ref[pl.ds(g_row + 3 * r + 2, 1), :]
        base_ref[r] = g_cmp * o_cmp[r] + (g_win / l_win) * o_win

    def slc_body(cw, carry):
        ms, ls = carry
        new_m, new_l = [], []
        for r, s in enumerate(slc_scores(cw)):
            m_new = jnp.maximum(ms[r], jnp.max(s, axis=0, keepdims=True))
            alpha = jnp.exp2(ms[r] - m_new)
            p = jnp.exp2(s - m_new)
            new_m.append(m_new)
            new_l.append(alpha * ls[r] + jnp.sum(p, axis=0, keepdims=True))
            acc_ref[r] = alpha * acc_ref[r] + pv(vst_ref, cw * (NSA_CW // NSA_KS), p,
                                                 NSA_CW // NSA_KS, NSA_KS)
        return tuple(new_m), tuple(new_l)

    n_steps = jnp.right_shift(q0 + (qb + NSA_CW - 1), NSA_CW.bit_length() - 1)
    _, ls = lax.fori_loop(1, n_steps, slc_body, (tuple(ms), tuple(ls)))

    for r in range(NSA_GROUP):
        g_slc = gt_ref[pl.ds(g_row + 3 * r + 1, 1), :]
        o = base_ref[r] + (g_slc / ls[r]) * acc_ref[r]
        o_ref[:, r * LANES:(r + 1) * LANES] = o.T.astype(o_ref.dtype)


def _nsa_attention(proj_n, kc, vct, proj_b, batch, seq):
    qb = NSA_QB
    nq = seq // qb
    n_chunks = seq // NSA_KC
    gw = NSA_GROUP * LANES
    kv_spec = lambda base: pl.BlockSpec((seq, LANES), lambda b, g, i: (b, base + g))
    return pl.pallas_call(
        functools.partial(_nsa_kernel, seq=seq),
        grid=(batch, NSA_KV_HEADS, nq),
        in_specs=[
            pl.BlockSpec((qb, gw), lambda b, g, i: (b * nq + i, g)),
            kv_spec(CB_KSLC), kv_spec(CB_KWIN), kv_spec(CB_VSLC), kv_spec(CB_VWIN),
            pl.BlockSpec((1, NB_PAD, LANES), lambda b, g, i: (g * batch + b, 0, 0)),
            pl.BlockSpec((1, LANES, NB_PAD), lambda b, g, i: (g * batch + b, 0, 0)),
            pl.BlockSpec((1, qb, LANES), lambda b, g, i: (B_GATE, b * nq + i, 0)),
        ],
        out_specs=pl.BlockSpec((qb, gw), lambda b, g, i: (b * nq + i, g)),
        out_shape=jax.ShapeDtypeStruct((batch * seq, NSA_HEADS * LANES), BF16),
        scratch_shapes=[
            pltpu.VMEM((seq // NSA_KS, LANES, NSA_KS), BF16),
            pltpu.VMEM((n_chunks, LANES, NSA_KC), BF16),
            pltpu.VMEM((NB_PAD, qb), F32),
            pltpu.VMEM((NB_PAD, qb), F32),
            pltpu.VMEM((LANES, qb), F32),
            pltpu.VMEM((NSA_GROUP, LANES, qb), F32),
            pltpu.VMEM((NSA_GROUP, LANES, qb), F32),
        ],
        compiler_params=_params(("arbitrary", "arbitrary", "arbitrary")),
        name="nsa_attention",
    )(proj_n, proj_n, proj_n, proj_n, proj_n, kc, vct, proj_b)


DIL_QB = 256


def _dil_bias(window, dil):
    rows = jnp.arange(DIL_QB, dtype=jnp.int32)[:, None]
    cols = jnp.arange(window + DIL_QB, dtype=jnp.int32)[None, :]
    delta = window + rows - cols
    ok = (delta >= 0) & (delta <= window) & (delta % dil == 0)
    return jnp.where(ok, 0.0, NEG).astype(F32)


def _dil_kernel(q0_ref, q1_ref, q2_ref, k0_ref, k1_ref, k2_ref, v0_ref, v1_ref, v2_ref,
                b0_ref, b1_ref, b2_ref, o0_ref, o1_ref, o2_ref,
                kp0_ref, kp1_ref, kp2_ref, vp0_ref, vp1_ref, vp2_ref, *, seq):
    i = pl.program_id(2)
    qs = (q0_ref, q1_ref, q2_ref)
    ks = (k0_ref, k1_ref, k2_ref)
    vs = (v0_ref, v1_ref, v2_ref)
    biases = (b0_ref, b1_ref, b2_ref)
    kps = (kp0_ref, kp1_ref, kp2_ref)
    vps = (vp0_ref, vp1_ref, vp2_ref)
    outs = (o0_ref, o1_ref, o2_ref)

    @pl.when(i == 0)
    def _():
        for p, (window, _) in enumerate(DIL_PAIRS):
            kps[p][0:window, :] = jnp.zeros((window, LANES), BF16)
            vps[p][0:window, :] = jnp.zeros((window, LANES), BF16)
            kps[p][window:window + seq, :] = ks[p][...]
            vps[p][window:window + seq, :] = vs[p][...]

    q_start = pl.multiple_of(i * DIL_QB, DIL_QB)
    accs, lses = [], []
    for p, (window, _) in enumerate(DIL_PAIRS):
        width = window + DIL_QB
        k_win = kps[p][pl.ds(q_start, width), :]
        v_win = vps[p][pl.ds(q_start, width), :]
        col = lax.broadcasted_iota(jnp.int32, (1, width), 1)
        in_seq = jnp.where(col >= window - q_start, 0.0, NEG)
        s = _nt_dot(qs[p][...], k_win) + biases[p][...] + in_seq
        m = jnp.max(s, axis=1, keepdims=True)
        e = jnp.exp2(s - m)
        l = jnp.sum(e, axis=1, keepdims=True)
        accs.append(_dot(e.astype(BF16), v_win) / l)
        lses.append(m + jnp.log2(l))

    top = jnp.maximum(jnp.maximum(lses[0], lses[1]), lses[2])
    ws = [jnp.exp2(x - top) for x in lses]
    inv = 1.0 / (ws[0] + ws[1] + ws[2])
    for p in range(len(DIL_PAIRS)):
        outs[p][...] = (accs[p] * (ws[p] * inv)).astype(outs[p].dtype)


def _dil_attention(proj_d, batch, seq):
    nq = seq // DIL_QB
    hp = DIL_HEADS_PER_PAIR

    def q_spec(p):
        return pl.BlockSpec((DIL_QB, LANES), lambda b, h, i: (b * nq + i, CB_DQ + p * hp + h))

    def kv_spec(base, p):
        return pl.BlockSpec((seq, LANES), lambda b, h, i: (b, base + p * hp + h))

    def bias_spec(p):
        return pl.BlockSpec((DIL_QB, DIL_PAIRS[p][0] + DIL_QB), lambda b, h, i: (0, 0))

    out_spec = pl.BlockSpec((DIL_QB, LANES), lambda b, h, i: (b * nq + i, h))
    out_shape = jax.ShapeDtypeStruct((batch * seq, hp * LANES), BF16)
    staged = [pltpu.VMEM((w + seq, LANES), BF16) for w, _ in DIL_PAIRS]
    return pl.pallas_call(
        functools.partial(_dil_kernel, seq=seq),
        grid=(batch, hp, nq),
        in_specs=[q_spec(0), q_spec(1), q_spec(2),
                  kv_spec(CB_DK, 0), kv_spec(CB_DK, 1), kv_spec(CB_DK, 2),
                  kv_spec(CB_DV, 0), kv_spec(CB_DV, 1), kv_spec(CB_DV, 2),
                  bias_spec(0), bias_spec(1), bias_spec(2)],
        out_specs=[out_spec, out_spec, out_spec],
        out_shape=[out_shape, out_shape, out_shape],
        scratch_shapes=staged + staged,
        compiler_params=_params(("arbitrary", "arbitrary", "arbitrary")),
        name="dilated_attention",
    )(*([proj_d] * 9), *[_dil_bias(w, d) for w, d in DIL_PAIRS])


SB_QB = 256
SB_PAST = 256
SB_KB = 128
SB_DONE = -151.0


def _sb_suffix(n):
    j = jnp.arange(n, dtype=jnp.int32)[:, None]
    s = jnp.arange(n, dtype=jnp.int32)[None, :]
    return jnp.where(j > s, 1.0, 0.0).astype(BF16)


SB_HP = 2


def _sb_kernel(q_ref, k_ref, v_ref, suffix_ref, o_ref):
    i = pl.program_id(2)
    width = SB_PAST + SB_QB

    def tile(q, k_chunk, v_chunk, suffix, carry_sum, before):
        z = _nt_dot(q, k_chunk)
        soft = jnp.log2(1.0 + jnp.exp2(-jnp.abs(z)))
        log_beta = jnp.minimum(z, 0.0) - soft
        log_1m = -jnp.maximum(z, 0.0) - soft
        if before is not None:
            log_1m = jnp.where(before, log_1m, 0.0)
        hi = log_1m.astype(BF16)
        lo = (log_1m - hi.astype(F32)).astype(BF16)
        logit = log_beta + (_dot(hi, suffix) + _dot(lo, suffix)) + carry_sum
        if before is not None:
            logit = jnp.where(before, logit, NEG)
        contrib = _dot(jnp.exp2(logit).astype(BF16), v_chunk)
        return contrib, carry_sum + jnp.sum(log_1m, axis=1, keepdims=True)

    q_start = i * SB_QB
    w_start = pl.multiple_of(jnp.maximum(q_start - SB_PAST, 0), SB_QB)
    offs = (lax.broadcasted_iota(jnp.int32, (SB_QB, width), 1)
            - lax.broadcasted_iota(jnp.int32, (SB_QB, width), 0))
    before = offs < q_start - w_start
    heads = [slice(h * LANES, (h + 1) * LANES) for h in range(SB_HP)]
    qs = [q_ref[:, sl] for sl in heads]
    state = [tile(qs[h], k_ref[pl.ds(w_start, width), heads[h]], v_ref[pl.ds(w_start, width), heads[h]],
                  suffix_ref[...], jnp.zeros((SB_QB, 1), F32), before) for h in range(SB_HP)]
    accs = tuple(s[0] for s in state)
    csums = tuple(s[1] for s in state)

    n_rest = jnp.right_shift(w_start, SB_KB.bit_length() - 1)
    suffix_kb = suffix_ref[0:SB_KB, 0:SB_KB]

    def live_of(csums):
        top = csums[0]
        for c in csums[1:]:
            top = jnp.maximum(top, c)
        return jnp.max(top)

    def cond(carry):
        c, _, _, live = carry
        return (c < n_rest) & (live > SB_DONE)

    def body(carry):
        c, accs, csums, _ = carry
        k_start = pl.multiple_of((n_rest - 1 - c) * SB_KB, SB_KB)
        new_a, new_c = [], []
        for h in range(SB_HP):
            contrib, cs = tile(qs[h], k_ref[pl.ds(k_start, SB_KB), heads[h]],
                               v_ref[pl.ds(k_start, SB_KB), heads[h]], suffix_kb, csums[h], None)
            new_a.append(accs[h] + contrib)
            new_c.append(cs)
        return c + 1, tuple(new_a), tuple(new_c), live_of(new_c)

    _, accs, _, _ = lax.while_loop(cond, body, (jnp.int32(0), accs, csums, live_of(csums)))
    for h in range(SB_HP):
        o_ref[:, heads[h]] = accs[h].astype(o_ref.dtype)


def _sb_attention(proj_d, batch, seq):
    nq = seq // SB_QB
    width = SB_PAST + SB_QB
    hw = SB_HP * LANES
    return pl.pallas_call(
        _sb_kernel,
        grid=(batch, SB_HEADS // SB_HP, nq),
        in_specs=[pl.BlockSpec((SB_QB, hw), lambda b, h, i: (b * nq + i, CB_SQ // SB_HP + h)),
                  pl.BlockSpec((seq, hw), lambda b, h, i: (b, CB_SK // SB_HP + h)),
                  pl.BlockSpec((seq, hw), lambda b, h, i: (b, CB_SV // SB_HP + h)),
                  pl.BlockSpec((width, width), lambda b, h, i: (0, 0))],
        out_specs=pl.BlockSpec((SB_QB, hw), lambda b, h, i: (b * nq + i, h)),
        out_shape=jax.ShapeDtypeStruct((batch * seq, SB_HEADS * LANES), BF16),
        compiler_params=_params(("parallel", "parallel", "arbitrary")),
        name="stick_breaking_attention",
    )(proj_d, proj_d, proj_d, _sb_suffix(width))


OUT_PIECES = (NSA_HEADS * LANES,) + (DIL_HEADS_PER_PAIR * LANES,) * 3 + (SB_HEADS * LANES,)


def _out_proj_kernel(a0, a1, a2, a3, a4, w_ref, x_ref, o_ref):
    acc = x_ref[...]
    off = 0
    for a_ref, width in zip((a0, a1, a2, a3, a4), OUT_PIECES):
        acc = acc + _dot(a_ref[...], w_ref[0, off:off + width, :].astype(BF16))
        off += width
    o_ref[...] = acc


def _out_proj(pieces, w3d, layer, x2d):
    m, n = x2d.shape
    tm, tn = 2048, 256
    a_specs = [pl.BlockSpec((tm, width), lambda i, j: (i, 0), pipeline_mode=pl.Buffered(1))
               for width in OUT_PIECES]
    return pl.pallas_call(
        _out_proj_kernel,
        grid=(m // tm, n // tn),
        in_specs=a_specs + [pl.BlockSpec((1, w3d.shape[1], tn), lambda i, j: (layer, 0, j)),
                            pl.BlockSpec((tm, tn), lambda i, j: (i, j))],
        out_specs=pl.BlockSpec((tm, tn), lambda i, j: (i, j)),
        out_shape=jax.ShapeDtypeStruct((m, n), F32),
        compiler_params=_params(("parallel", "arbitrary")),
        name="out_proj",
    )(*pieces, w3d, x2d)


def _mlp_up_kernel(a_ref, w_ref, o_ref):
    r = jnp.maximum(_dot(a_ref[...], w_ref[0].astype(BF16)), 0.0)
    o_ref[...] = (r * r).astype(o_ref.dtype)


def _mlp_up(h, w3d, layer):
    m, k = h.shape
    n = w3d.shape[2]
    tm, tn = 2048, 512
    return pl.pallas_call(
        _mlp_up_kernel,
        grid=(m // tm, n // tn),
        in_specs=[pl.BlockSpec((tm, k), lambda i, j: (i, 0), pipeline_mode=pl.Buffered(1)),
                  pl.BlockSpec((1, k, tn), lambda i, j: (layer, 0, j))],
        out_specs=pl.BlockSpec((tm, tn), lambda i, j: (i, j)),
        out_shape=jax.ShapeDtypeStruct((m, n), BF16),
        compiler_params=_params(("parallel", "arbitrary")),
        name="mlp_up",
    )(h, w3d)


def _mlp_down_kernel(a_ref, w_ref, x_ref, o_ref):
    @pl.when(pl.program_id(2) == 0)
    def _():
        o_ref[...] = x_ref[...]

    o_ref[...] += _dot(a_ref[...], w_ref[0])


def _mlp_down(a, w3d, layer, x2d):
    m, kdim = a.shape
    n = w3d.shape[2]
    tm, tn, tk = 1024, 1024, 2048
    return pl.pallas_call(
        _mlp_down_kernel,
        grid=(m // tm, n // tn, kdim // tk),
        in_specs=[pl.BlockSpec((tm, tk), lambda i, j, k: (i, k)),
                  pl.BlockSpec((1, tk, tn), lambda i, j, k: (layer, k, j)),
                  pl.BlockSpec((tm, tn), lambda i, j, k: (i, j))],
        out_specs=pl.BlockSpec((tm, tn), lambda i, j, k: (i, j)),
        out_shape=jax.ShapeDtypeStruct((m, n), F32),
        compiler_params=_params(("parallel", "parallel", "arbitrary")),
        name="mlp_down",
    )(a, w3d, x2d)


def _rope_tables(seq):
    inv = 1.0 / (ROPE_THETA ** (jnp.arange(0, HEAD_DIM, 2, dtype=F32) / HEAD_DIM))
    ang = jnp.arange(seq, dtype=F32)[:, None] * inv[None, :]
    cos, sin = jnp.cos(ang), jnp.sin(ang)
    return jnp.concatenate([cos, cos], axis=-1), jnp.concatenate([-sin, sin], axis=-1)


def _nsa_mixer(proj_n, proj_b, batch, seq, pe_k, pe_v, w_ck1, w_ck2, w_cv1, w_cv2):
    nb = seq // NSA_BLOCK
    groups = NSA_KV_HEADS
    x_kv = proj_b[:B_GATE].reshape(2 * groups * batch * nb, NSA_BLOCK * HEAD_DIM)
    pe = jnp.stack([pe_k.reshape(1, -1), pe_v.reshape(1, -1)])
    w1 = jnp.stack([w_ck1, w_cv1]).astype(BF16)
    w2 = jnp.stack([w_ck2, w_cv2]).astype(BF16)
    cmp_kv = _compress(x_kv, pe, w1, w2)
    cmp_kv = cmp_kv.reshape(2, groups * batch, nb, HEAD_DIM)
    cmp_kv = jnp.pad(cmp_kv, ((0, 0), (0, 0), (0, NB_PAD - nb), (0, 0))).astype(BF16)
    kc = cmp_kv[0]
    vct = cmp_kv[1].transpose(0, 2, 1)
    return _nsa_attention(proj_n, kc, vct, proj_b, batch, seq)


def _layer(x2d, batch, seq, cosf, sinf, layer, w_in, w_tail, g_mix, w_out, g_mlp, w_up, w_down,
           pe_k, pe_v, w_ck1, w_ck2, w_cv1, w_cv2):
    h = _rmsnorm(x2d, g_mix, BF16)
    proj_n = _proj(h, w_in, layer, lambda j: jnp.where(j >= 3, j + 1, j), N_COLS // PROJ_TN, 3, 2,
                   ((0, 2, SCALE_LOG2E),), cosf, sinf, seq)
    q_tiles = DIL_HEADS * LANES // PROJ_TN
    sq_tile = CB_SQ * LANES // PROJ_TN
    proj_d = _proj(h, w_tail, layer, lambda j: j, A_COLS // PROJ_TN, 2 * q_tiles, 0,
                   ((0, q_tiles - 1, SCALE_LOG2E),
                    (sq_tile, sq_tile + SB_HEADS * LANES // PROJ_TN - 1, SCALE_LOG2E)),
                   cosf, sinf, seq)
    proj_b = _proj_b(h, w_in, layer, cosf, sinf, seq)

    o_a = _nsa_mixer(proj_n, proj_b, batch, seq, pe_k, pe_v, w_ck1, w_ck2, w_cv1, w_cv2)
    o_b = _dil_attention(proj_d, batch, seq)
    o_c = _sb_attention(proj_d, batch, seq)

    x2d = _out_proj([o_a, *o_b, o_c], w_out, layer, x2d)
    h2 = _rmsnorm(x2d, g_mlp, BF16)
    up = _mlp_up(h2, w_up, layer)
    return _mlp_down(up, w_down, layer, x2d)


def kernel(x, w_in, w_out, norm_mix, norm_mlp, w_up, w_down, nsa_pe_k, nsa_pe_v,
           nsa_w_ck1, nsa_w_ck2, nsa_w_cv1, nsa_w_cv2, final_norm):
    batch, seq, d = x.shape
    cosf, sinf = _rope_tables(seq)
    depth = w_in.shape[0]
    w_head = jnp.transpose(w_in[:, :, :W_GATE + LANES], (0, 2, 1)).astype(BF16)
    w_tail = jnp.transpose(w_in[:, :, W_TAIL:], (0, 2, 1)).astype(BF16)
    w_down = w_down.astype(BF16)
    x2d = x.reshape(batch * seq, d)
    for layer in range(depth):
        x2d = _layer(x2d, batch, seq, cosf, sinf, layer, w_head, w_tail, norm_mix[layer], w_out,
                     norm_mlp[layer], w_up, w_down, nsa_pe_k[layer], nsa_pe_v[layer],
                     nsa_w_ck1[layer], nsa_w_ck2[layer], nsa_w_cv1[layer], nsa_w_cv2[layer])
    return _rmsnorm(x2d, final_norm, x.dtype).reshape(batch, seq, d)
```

```python
import functools

import jax
import jax.numpy as jnp
from jax import lax
from jax.experimental import pallas as pl
from jax.experimental.pallas import tpu as pltpu

F32 = jnp.float32
BF16 = jnp.bfloat16

HEAD_DIM = 128
ROPE_THETA = 10000.0
EPS = 1e-6
NEG = -1e30
NSA_HEADS = 12
NSA_KV_HEADS = 2
NSA_GROUP = NSA_HEADS // NSA_KV_HEADS
NSA_BLOCK = 64
NSA_TOPN = 16
NSA_WINDOW = 512
NSA_FORCED_SCORE = NSA_GROUP + 1.0
DIL_PAIRS = ((128, 1), (512, 4), (2048, 16))
DIL_HEADS_PER_PAIR = 4
DIL_HEADS = DIL_HEADS_PER_PAIR * len(DIL_PAIRS)
SB_HEADS = 8
SCALE_LOG2E = HEAD_DIM ** -0.5 * 1.4426950408889634

LANES = 128
VMEM_LIMIT = 56 * 1024 * 1024

KV_W = NSA_KV_HEADS * HEAD_DIM
W_KCMP = NSA_HEADS * HEAD_DIM
W_GATE = W_KCMP + 6 * KV_W
N_GATE = NSA_HEADS * 3
W_TAIL = W_GATE + N_GATE
PROJ_TN = 512
N_COLS = 20 * LANES
CB_NSA_Q, CB_KSLC, CB_VSLC, CB_KWIN, CB_VWIN = 0, 12, 14, 16, 18
A_COLS = 60 * LANES
CB_DQ, CB_DK, CB_DV = 0, 12, 24
CB_SQ, CB_SK, CB_SV = 36, 44, 52
B_CHUNKS = 5
B_GATE = 4


def _params(sem):
    return pltpu.CompilerParams(dimension_semantics=sem, vmem_limit_bytes=VMEM_LIMIT)


def _nt_dot(a, b):
    return lax.dot_general(a, b, (((1,), (1,)), ((), ())), preferred_element_type=F32)


def _dot(a, b):
    return jnp.dot(a, b, preferred_element_type=F32)


def _norm_rows(x, g):
    ms = jnp.mean(x * x, axis=-1, keepdims=True)
    return x * lax.rsqrt(ms + EPS) * g


def _norm_into(x_ref, g_ref, h_ref, rows=256):
    for r in range(0, x_ref.shape[0], rows):
        h_ref[r:r + rows, :] = _norm_rows(x_ref[r:r + rows, :], g_ref[...]).astype(h_ref.dtype)


def _rmsnorm_kernel(x_ref, g_ref, o_ref):
    o_ref[...] = _norm_rows(x_ref[...], g_ref[...]).astype(o_ref.dtype)


def _rmsnorm(x2d, g, out_dtype):
    m, d = x2d.shape
    tm = 256
    return pl.pallas_call(
        _rmsnorm_kernel,
        grid=(m // tm,),
        in_specs=[pl.BlockSpec((tm, d), lambda i: (i, 0)),
                  pl.BlockSpec((1, d), lambda i: (0, 0))],
        out_specs=pl.BlockSpec((tm, d), lambda i: (i, 0)),
        out_shape=jax.ShapeDtypeStruct((m, d), out_dtype),
        compiler_params=_params(("parallel",)),
        name="rmsnorm",
    )(x2d, g.reshape(1, d))


def _rope(xc, cos, sin):
    return xc * cos + pltpu.roll(xc, HEAD_DIM // 2, 1) * sin


def _proj_kernel(a_ref, w_ref, cos_ref, sin_ref, o_ref, *, n_full, n_half, q_scales):
    acc = _nt_dot(a_ref[...], w_ref[0].astype(BF16))
    _proj_epilogue(acc, cos_ref, sin_ref, o_ref, n_full, n_half, q_scales)


def _proj_norm_kernel(x_ref, g_ref, w_ref, cos_ref, sin_ref, o_ref, h_ref, *, n_full, n_half, q_scales):
    @pl.when(pl.program_id(1) == 0)
    def _():
        _norm_into(x_ref, g_ref, h_ref)

    acc = _nt_dot(h_ref[...], w_ref[0].astype(BF16))
    _proj_epilogue(acc, cos_ref, sin_ref, o_ref, n_full, n_half, q_scales)


def _proj_epilogue(acc, cos_ref, sin_ref, o_ref, n_full, n_half, q_scales):
    j = pl.program_id(1)
    factor = jnp.float32(1.0)
    for lo, hi, f in q_scales:
        factor = jnp.where((j >= lo) & (j <= hi), jnp.float32(f), factor)

    n_roped = jnp.where(j < n_full, PROJ_TN // LANES, jnp.where(j < n_full + n_half, NSA_KV_HEADS, 0))
    cos_f = cos_ref[...] * factor
    sin_f = sin_ref[...] * factor
    for c in range(PROJ_TN // LANES):
        sl = slice(c * LANES, (c + 1) * LANES)
        roped = n_roped > c
        o_ref[:, sl] = _rope(acc[:, sl], jnp.where(roped, cos_f, factor),
                             jnp.where(roped, sin_f, 0.0)).astype(o_ref.dtype)


def _proj(h, w3d, layer, src_tile, n_tiles, n_full, n_half, q_scales, cosf, sinf, seq):
    m, k = h.shape
    tm = 1024
    t_blocks = seq // tm
    return pl.pallas_call(
        functools.partial(_proj_kernel, n_full=n_full, n_half=n_half, q_scales=q_scales),
        grid=(m // tm, n_tiles),
        in_specs=[pl.BlockSpec((tm, k), lambda i, j: (i, 0)),
                  pl.BlockSpec((1, PROJ_TN, k), lambda i, j: (layer, src_tile(j), 0)),
                  pl.BlockSpec((tm, LANES), lambda i, j: (i % t_blocks, 0)),
                  pl.BlockSpec((tm, LANES), lambda i, j: (i % t_blocks, 0))],
        out_specs=pl.BlockSpec((tm, PROJ_TN), lambda i, j: (i, j)),
        out_shape=jax.ShapeDtypeStruct((m, n_tiles * PROJ_TN), BF16),
        compiler_params=_params(("parallel", "arbitrary")),
        name="proj",
    )(h, w3d, cosf, sinf)


def _proj_norm(x2d, g, w3d, layer, n_tiles, n_full, n_half, q_scales, cosf, sinf, seq):
    m, k = x2d.shape
    tm = 1024
    t_blocks = seq // tm
    return pl.pallas_call(
        functools.partial(_proj_norm_kernel, n_full=n_full, n_half=n_half, q_scales=q_scales),
        grid=(m // tm, n_tiles),
        in_specs=[pl.BlockSpec((tm, k), lambda i, j: (i, 0), pipeline_mode=pl.Buffered(1)),
                  pl.BlockSpec((1, k), lambda i, j: (0, 0)),
                  pl.BlockSpec((1, PROJ_TN, k), lambda i, j: (layer, j, 0)),
                  pl.BlockSpec((tm, LANES), lambda i, j: (i % t_blocks, 0)),
                  pl.BlockSpec((tm, LANES), lambda i, j: (i % t_blocks, 0))],
        out_specs=[pl.BlockSpec((tm, PROJ_TN), lambda i, j: (i, j)),
                   pl.BlockSpec((tm, k), lambda i, j: (i, 0))],
        out_shape=[jax.ShapeDtypeStruct((m, n_tiles * PROJ_TN), BF16),
                   jax.ShapeDtypeStruct((m, k), BF16)],
        compiler_params=_params(("parallel", "arbitrary")),
        name="proj_norm",
    )(x2d, g.reshape(1, k), w3d, cosf, sinf)


def _proj_b_kernel(a_ref, w_ref, wg_ref, cos_ref, sin_ref, o_ref):
    a = a_ref[...]
    acc = _nt_dot(a, w_ref[0].astype(BF16))
    for c in range(B_GATE):
        xc = acc[:, c * LANES:(c + 1) * LANES]
        o_ref[c] = _rope(xc, cos_ref[...], sin_ref[...]) if c < NSA_KV_HEADS else xc
    o_ref[B_GATE] = _nt_dot(a, wg_ref[0].astype(BF16))


def _proj_b(h, w3d, layer, cosf, sinf, seq):
    m, k = h.shape
    tm = 1024
    t_blocks = seq // tm
    return pl.pallas_call(
        _proj_b_kernel,
        grid=(m // tm,),
        in_specs=[pl.BlockSpec((tm, k), lambda i: (i, 0)),
                  pl.BlockSpec((1, 2 * KV_W, k), lambda i: (layer, W_KCMP // (2 * KV_W), 0)),
                  pl.BlockSpec((1, LANES, k), lambda i: (layer, W_GATE // LANES, 0)),
                  pl.BlockSpec((tm, LANES), lambda i: (i % t_blocks, 0)),
                  pl.BlockSpec((tm, LANES), lambda i: (i % t_blocks, 0))],
        out_specs=pl.BlockSpec((B_CHUNKS, tm, LANES), lambda i: (0, i, 0)),
        out_shape=jax.ShapeDtypeStruct((B_CHUNKS, m, LANES), F32),
        compiler_params=_params(("parallel",)),
        name="proj_b",
    )(h, w3d, w3d, cosf, sinf)


def _compress_kernel(x_ref, pe_ref, w1_ref, w2_ref, o_ref):
    blk = (x_ref[...] + pe_ref[0]).astype(BF16)
    hid = _dot(blk, w1_ref[0])
    hid = hid * jax.nn.sigmoid(hid)
    o_ref[0] = _dot(hid.astype(BF16), w2_ref[0])


def _compress(x_kv, pe, w1, w2):
    rows = x_kv.shape[0] // 2
    kdim = x_kv.shape[1]
    hidden = w1.shape[-1]
    return pl.pallas_call(
        _compress_kernel,
        grid=(2,),
        in_specs=[pl.BlockSpec((rows, kdim), lambda i: (i, 0)),
                  pl.BlockSpec((1, 1, kdim), lambda i: (i, 0, 0)),
                  pl.BlockSpec((1, kdim, hidden), lambda i: (i, 0, 0)),
                  pl.BlockSpec((1, hidden, HEAD_DIM), lambda i: (i, 0, 0))],
        out_specs=pl.BlockSpec((1, rows, HEAD_DIM), lambda i: (i, 0, 0)),
        out_shape=jax.ShapeDtypeStruct((2, rows, HEAD_DIM), F32),
        compiler_params=_params(("parallel",)),
        name="nsa_compress",
    )(x_kv, pe, w1, w2)


NSA_QB = 256
NSA_KC = 128
NSA_KS = 256
NSA_CW = 512
NSA_WIN_KEYS = NSA_WINDOW + NSA_QB
NB_PAD = 128


def _nsa_kernel(q_ref, ks_ref, kw_ref, vs_ref, vw_ref, kc_ref, vct_ref, gate_ref, o_ref,
                vst_ref, vwt_ref, val_ref, sel_ref, gt_ref, acc_ref, base_ref, *, seq):
    i = pl.program_id(2)
    qb = NSA_QB
    n_blocks = seq // NSA_BLOCK
    q0 = pl.multiple_of(i * qb, qb)

    @pl.when(i == 0)
    def _():
        for c in range(seq // NSA_KC):
            rows = slice(c * NSA_KC, (c + 1) * NSA_KC)
            vwt_ref[c] = vw_ref[rows, :].astype(F32).T.astype(BF16)
        for c in range(seq // NSA_KS):
            rows = slice(c * NSA_KS, (c + 1) * NSA_KS)
            vst_ref[c] = vs_ref[rows, :].astype(F32).T.astype(BF16)

    qs = [q_ref[:, r * LANES:(r + 1) * LANES] for r in range(NSA_GROUP)]
    t_row = i * qb + lax.broadcasted_iota(jnp.int32, (1, qb), 1)
    sub_n = lax.broadcasted_iota(jnp.int32, (NB_PAD, qb), 0)
    t_full = i * qb + lax.broadcasted_iota(jnp.int32, (NB_PAD, qb), 1)

    cmp_ok = (sub_n * NSA_BLOCK + (NSA_BLOCK - 1) <= t_full) & (sub_n < n_blocks)
    any_ok = jnp.where(t_row >= NSA_BLOCK - 1, 1.0, 0.0)
    kc = kc_ref[0]
    vct = vct_ref[0]
    imp = jnp.zeros((NB_PAD, qb), F32)
    o_cmp = []
    for r in range(NSA_GROUP):
        s = jnp.where(cmp_ok, _nt_dot(kc, qs[r]), NEG)
        e = jnp.exp2(s - jnp.max(s, axis=0, keepdims=True))
        p = e * (any_ok / jnp.sum(e, axis=0, keepdims=True))
        imp = imp + p
        o_cmp.append(_dot(vct, p.astype(BF16)))

    cur = jnp.right_shift(t_full, NSA_BLOCK.bit_length() - 1)
    started = sub_n <= cur
    forced = (sub_n == 0) | (sub_n == cur) | (sub_n == cur - 1)
    val = jnp.where(forced, NSA_FORCED_SCORE, jnp.where(started, imp, -1.0))
    val_ref[...] = val
    val_n = val[:n_blocks]
    sub_nn = lax.broadcasted_iota(jnp.int32, (n_blocks, qb), 0)
    rank = jnp.zeros((n_blocks, qb), F32)
    for n2 in range(n_blocks):
        row = val_ref[n2:n2 + 1, :]
        rank = rank + jnp.where(sub_nn > n2, jnp.where(row >= val_n, 1.0, 0.0),
                                jnp.where(row > val_n, 1.0, 0.0))
    sel_ref[0:n_blocks, :] = jnp.where(rank < NSA_TOPN, 1.0, 0.0)

    per_step = NSA_CW // NSA_BLOCK
    d_cw = (lax.broadcasted_iota(jnp.int32, (NSA_CW, qb), 0)
            - lax.broadcasted_iota(jnp.int32, (NSA_CW, qb), 1))

    def slc_scores(cw):
        k0 = cw * NSA_CW if isinstance(cw, int) else pl.multiple_of(cw * NSA_CW, NSA_CW)
        k_chunk = ks_ref[pl.ds(k0, NSA_CW), :]
        picked = jnp.concatenate(
            [jnp.broadcast_to(sel_ref[pl.ds(per_step * cw + b, 1), :], (NSA_BLOCK, qb))
             for b in range(per_step)], axis=0)
        mask = jnp.where(d_cw <= q0 - k0, picked, 0.0) > 0.5
        return [jnp.where(mask, _nt_dot(k_chunk, qs[r]), NEG) for r in range(NSA_GROUP)]

    def pv(vt_ref, first_chunk, p, n_chunks, kc):
        out = None
        for j in range(n_chunks):
            part = _dot(vt_ref[first_chunk + j], p[j * kc:(j + 1) * kc].astype(BF16))
            out = part if out is None else out + part
        return out

    ms, ls = [], []
    for r, s in enumerate(slc_scores(0)):
        m = jnp.max(s, axis=0, keepdims=True)
        p = jnp.exp2(s - m)
        ms.append(m)
        ls.append(jnp.sum(p, axis=0, keepdims=True))
        acc_ref[r] = pv(vst_ref, 0, p, NSA_CW // NSA_KS, NSA_KS)

    w_start = pl.multiple_of(jnp.maximum(q0 - NSA_WINDOW, 0), NSA_KC)
    delta = ((q0 - w_start) + lax.broadcasted_iota(jnp.int32, (NSA_WIN_KEYS, qb), 1)
             - lax.broadcasted_iota(jnp.int32, (NSA_WIN_KEYS, qb), 0))
    in_win = (delta >= 0) & (delta < NSA_WINDOW)
    k_win = kw_ref[pl.ds(w_start, NSA_WIN_KEYS), :]
    first_chunk = jnp.right_shift(w_start, NSA_KC.bit_length() - 1)
    gt_ref[...] = jax.nn.sigmoid(gate_ref[0].T)
    g_row = pl.program_id(1) * (NSA_GROUP * 3)
    for r in range(NSA_GROUP):
        s = jnp.where(in_win, _nt_dot(k_win, qs[r]), NEG)
        p = jnp.exp2(s - jnp.max(s, axis=0, keepdims=True))
        l_win = jnp.sum(p, axis=0, keepdims=True)
        o_win = pv(vwt_ref, first_chunk, p, NSA_WIN_KEYS // NSA_KC, NSA_KC)
        g_cmp = gt_ref[pl.ds(g_row + 3 * r, 1), :]
        g_win = gt_ref[pl.ds(g_row + 3 * r + 2, 1), :]
        base_ref[r] = g_cmp * o_cmp[r] + (g_win / l_win) * o_win

    def slc_body(cw, carry):
        ms, ls = carry
        new_m, new_l = [], []
        for r, s in enumerate(slc_scores(cw)):
            m_new = jnp.maximum(ms[r], jnp.max(s, axis=0, keepdims=True))
            alpha = jnp.exp2(ms[r] - m_new)
            p = jnp.exp2(s - m_new)
            new_m.append(m_new)
            new_l.append(alpha * ls[r] + jnp.sum(p, axis=0, keepdims=True))
            acc_ref[r] = alpha * acc_ref[r] + pv(vst_ref, cw * (NSA_CW // NSA_KS), p,
                                                 NSA_CW // NSA_KS, NSA_KS)
        return tuple(new_m), tuple(new_l)

    n_steps = jnp.right_shift(q0 + (qb + NSA_CW - 1), NSA_CW.bit_length() - 1)
    _, ls = lax.fori_loop(1, n_steps, slc_body, (tuple(ms), tuple(ls)))

    for r in range(NSA_GROUP):
        g_slc = gt_ref[pl.ds(g_row + 3 * r + 1, 1), :]
        o = base_ref[r] + (g_slc / ls[r]) * acc_ref[r]
        o_ref[:, r * LANES:(r + 1) * LANES] = o.T.astype(o_ref.dtype)


def _nsa_attention(proj_n, kc, vct, proj_b, batch, seq):
    qb = NSA_QB
    nq = seq // qb
    n_chunks = seq // NSA_KC
    gw = NSA_GROUP * LANES
    kv_spec = lambda base: pl.BlockSpec((seq, LANES), lambda b, g, i: (b, base + g))
    return pl.pallas_call(
        functools.partial(_nsa_kernel, seq=seq),
        grid=(batch, NSA_KV_HEADS, nq),
        in_specs=[
            pl.BlockSpec((qb, gw), lambda b, g, i: (b * nq + i, g)),
            kv_spec(CB_KSLC), kv_spec(CB_KWIN), kv_spec(CB_VSLC), kv_spec(CB_VWIN),
            pl.BlockSpec((1, NB_PAD, LANES), lambda b, g, i: (g * batch + b, 0, 0)),
            pl.BlockSpec((1, LANES, NB_PAD), lambda b, g, i: (g * batch + b, 0, 0)),
            pl.BlockSpec((1, qb, LANES), lambda b, g, i: (B_GATE, b * nq + i, 0)),
        ],
        out_specs=pl.BlockSpec((qb, gw), lambda b, g, i: (b * nq + i, g)),
        out_shape=jax.ShapeDtypeStruct((batch * seq, NSA_HEADS * LANES), BF16),
        scratch_shapes=[
            pltpu.VMEM((seq // NSA_KS, LANES, NSA_KS), BF16),
            pltpu.VMEM((n_chunks, LANES, NSA_KC), BF16),
            pltpu.VMEM((NB_PAD, qb), F32),
            pltpu.VMEM((NB_PAD, qb), F32),
            pltpu.VMEM((LANES, qb), F32),
            pltpu.VMEM((NSA_GROUP, LANES, qb), F32),
            pltpu.VMEM((NSA_GROUP, LANES, qb), F32),
        ],
        compiler_params=_params(("arbitrary", "arbitrary", "arbitrary")),
        name="nsa_attention",
    )(proj_n, proj_n, proj_n, proj_n, proj_n, kc, vct, proj_b)


DIL_QB = 256


def _dil_bias(window, dil):
    rows = jnp.arange(DIL_QB, dtype=jnp.int32)[:, None]
    cols = jnp.arange(window + DIL_QB, dtype=jnp.int32)[None, :]
    delta = window + rows - cols
    ok = (delta >= 0) & (delta <= window) & (delta % dil == 0)
    return jnp.where(ok, 0.0, NEG).astype(F32)


def _dil_kernel(q0_ref, q1_ref, q2_ref, k0_ref, k1_ref, k2_ref, v0_ref, v1_ref, v2_ref,
                b0_ref, b1_ref, b2_ref, o0_ref, o1_ref, o2_ref,
                kp0_ref, kp1_ref, kp2_ref, vp0_ref, vp1_ref, vp2_ref, *, seq):
    i = pl.program_id(2)
    qs = (q0_ref, q1_ref, q2_ref)
    ks = (k0_ref, k1_ref, k2_ref)
    vs = (v0_ref, v1_ref, v2_ref)
    biases = (b0_ref, b1_ref, b2_ref)
    kps = (kp0_ref, kp1_ref, kp2_ref)
    vps = (vp0_ref, vp1_ref, vp2_ref)
    outs = (o0_ref, o1_ref, o2_ref)

    @pl.when(i == 0)
    def _():
        for p, (window, _) in enumerate(DIL_PAIRS):
            kps[p][0:window, :] = jnp.zeros((window, LANES), BF16)
            vps[p][0:window, :] = jnp.zeros((window, LANES), BF16)
            kps[p][window:window + seq, :] = ks[p][...]
            vps[p][window:window + seq, :] = vs[p][...]

    q_start = pl.multiple_of(i * DIL_QB, DIL_QB)
    accs, lses = [], []
    for p, (window, _) in enumerate(DIL_PAIRS):
        width = window + DIL_QB
        k_win = kps[p][pl.ds(q_start, width), :]
        v_win = vps[p][pl.ds(q_start, width), :]
        col = lax.broadcasted_iota(jnp.int32, (1, width), 1)
        in_seq = jnp.where(col >= window - q_start, 0.0, NEG)
        s = _nt_dot(qs[p][...], k_win) + biases[p][...] + in_seq
        m = jnp.max(s, axis=1, keepdims=True)
        e = jnp.exp2(s - m)
        l = jnp.sum(e, axis=1, keepdims=True)
        accs.append(_dot(e.astype(BF16), v_win) / l)
        lses.append(m + jnp.log2(l))

    top = jnp.maximum(jnp.maximum(lses[0], lses[1]), lses[2])
    ws = [jnp.exp2(x - top) for x in lses]
    inv = 1.0 / (ws[0] + ws[1] + ws[2])
    for p in range(len(DIL_PAIRS)):
        outs[p][...] = (accs[p] * (ws[p] * inv)).astype(outs[p].dtype)


def _dil_attention(proj_d, batch, seq):
    nq = seq // DIL_QB
    hp = DIL_HEADS_PER_PAIR

    def q_spec(p):
        return pl.BlockSpec((DIL_QB, LANES), lambda b, h, i: (b * nq + i, CB_DQ + p * hp + h))

    def kv_spec(base, p):
        return pl.BlockSpec((seq, LANES), lambda b, h, i: (b, base + p * hp + h))

    def bias_spec(p):
        return pl.BlockSpec((DIL_QB, DIL_PAIRS[p][0] + DIL_QB), lambda b, h, i: (0, 0))

    out_spec = pl.BlockSpec((DIL_QB, LANES), lambda b, h, i: (b * nq + i, h))
    out_shape = jax.ShapeDtypeStruct((batch * seq, hp * LANES), BF16)
    staged = [pltpu.VMEM((w + seq, LANES), BF16) for w, _ in DIL_PAIRS]
    return pl.pallas_call(
        functools.partial(_dil_kernel, seq=seq),
        grid=(batch, hp, nq),
        in_specs=[q_spec(0), q_spec(1), q_spec(2),
                  kv_spec(CB_DK, 0), kv_spec(CB_DK, 1), kv_spec(CB_DK, 2),
                  kv_spec(CB_DV, 0), kv_spec(CB_DV, 1), kv_spec(CB_DV, 2),
                  bias_spec(0), bias_spec(1), bias_spec(2)],
        out_specs=[out_spec, out_spec, out_spec],
        out_shape=[out_shape, out_shape, out_shape],
        scratch_shapes=staged + staged,
        compiler_params=_params(("arbitrary", "arbitrary", "arbitrary")),
        name="dilated_attention",
    )(*([proj_d] * 9), *[_dil_bias(w, d) for w, d in DIL_PAIRS])


SB_QB = 256
SB_PAST = 256
SB_KB = 128
SB_DONE = -151.0


def _sb_suffix(n):
    j = jnp.arange(n, dtype=jnp.int32)[:, None]
    s = jnp.arange(n, dtype=jnp.int32)[None, :]
    return jnp.where(j > s, 1.0, 0.0).astype(BF16)


SB_HP = 2


def _sb_kernel(q_ref, k_ref, v_ref, suffix_ref, o_ref):
    i = pl.program_id(2)
    width = SB_PAST + SB_QB

    def tile(q, k_chunk, v_chunk, suffix, carry_sum, before):
        z = _nt_dot(q, k_chunk)
        soft = jnp.log2(1.0 + jnp.exp2(-jnp.abs(z)))
        log_beta = jnp.minimum(z, 0.0) - soft
        log_1m = -jnp.maximum(z, 0.0) - soft
        if before is not None:
            log_1m = jnp.where(before, log_1m, 0.0)
        hi = log_1m.astype(BF16)
        lo = (log_1m - hi.astype(F32)).astype(BF16)
        logit = log_beta + (_dot(hi, suffix) + _dot(lo, suffix)) + carry_sum
        if before is not None:
            logit = jnp.where(before, logit, NEG)
        contrib = _dot(jnp.exp2(logit).astype(BF16), v_chunk)
        return contrib, carry_sum + jnp.sum(log_1m, axis=1, keepdims=True)

    q_start = i * SB_QB
    w_start = pl.multiple_of(jnp.maximum(q_start - SB_PAST, 0), SB_QB)
    offs = (lax.broadcasted_iota(jnp.int32, (SB_QB, width), 1)
            - lax.broadcasted_iota(jnp.int32, (SB_QB, width), 0))
    before = offs < q_start - w_start
    heads = [slice(h * LANES, (h + 1) * LANES) for h in range(SB_HP)]
    qs = [q_ref[:, sl] for sl in heads]
    state = [tile(qs[h], k_ref[pl.ds(w_start, width), heads[h]], v_ref[pl.ds(w_start, width), heads[h]],
                  suffix_ref[...], jnp.zeros((SB_QB, 1), F32), before) for h in range(SB_HP)]
    accs = tuple(s[0] for s in state)
    csums = tuple(s[1] for s in state)

    n_rest = jnp.right_shift(w_start, SB_KB.bit_length() - 1)
    suffix_kb = suffix_ref[0:SB_KB, 0:SB_KB]

    def live_of(csums):
        top = csums[0]
        for c in csums[1:]:
            top = jnp.maximum(top, c)
        return jnp.max(top)

    def cond(carry):
        c, _, _, live = carry
        return (c < n_rest) & (live > SB_DONE)

    def body(carry):
        c, accs, csums, _ = carry
        k_start = pl.multiple_of((n_rest - 1 - c) * SB_KB, SB_KB)
        new_a, new_c = [], []
        for h in range(SB_HP):
            contrib, cs = tile(qs[h], k_ref[pl.ds(k_start, SB_KB), heads[h]],
                               v_ref[pl.ds(k_start, SB_KB), heads[h]], suffix_kb, csums[h], None)
            new_a.append(accs[h] + contrib)
            new_c.append(cs)
        return c + 1, tuple(new_a), tuple(new_c), live_of(new_c)

    _, accs, _, _ = lax.while_loop(cond, body, (jnp.int32(0), accs, csums, live_of(csums)))
    for h in range(SB_HP):
        o_ref[:, heads[h]] = accs[h].astype(o_ref.dtype)


def _sb_attention(proj_d, batch, seq):
    nq = seq // SB_QB
    width = SB_PAST + SB_QB
    hw = SB_HP * LANES
    return pl.pallas_call(
        _sb_kernel,
        grid=(batch, SB_HEADS // SB_HP, nq),
        in_specs=[pl.BlockSpec((SB_QB, hw), lambda b, h, i: (b * nq + i, CB_SQ // SB_HP + h)),
                  pl.BlockSpec((seq, hw), lambda b, h, i: (b, CB_SK // SB_HP + h)),
                  pl.BlockSpec((seq, hw), lambda b, h, i: (b, CB_SV // SB_HP + h)),
                  pl.BlockSpec((width, width), lambda b, h, i: (0, 0))],
        out_specs=pl.BlockSpec((SB_QB, hw), lambda b, h, i: (b * nq + i, h)),
        out_shape=jax.ShapeDtypeStruct((batch * seq, SB_HEADS * LANES), BF16),
        compiler_params=_params(("parallel", "parallel", "arbitrary")),
        name="stick_breaking_attention",
    )(proj_d, proj_d, proj_d, _sb_suffix(width))


OUT_PIECES = (NSA_HEADS * LANES,) + (DIL_HEADS_PER_PAIR * LANES,) * 3 + (SB_HEADS * LANES,)


def _out_proj_kernel(a0, a1, a2, a3, a4, w_ref, x_ref, o_ref):
    acc = x_ref[...]
    off = 0
    for a_ref, width in zip((a0, a1, a2, a3, a4), OUT_PIECES):
        acc = acc + _dot(a_ref[...], w_ref[0, off:off + width, :].astype(BF16))
        off += width
    o_ref[...] = acc


def _out_proj(pieces, w3d, layer, x2d):
    m, n = x2d.shape
    tm, tn = 1024, 512
    a_specs = [pl.BlockSpec((tm, width), lambda i, j: (i, 0)) for width in OUT_PIECES]
    return pl.pallas_call(
        _out_proj_kernel,
        grid=(m // tm, n // tn),
        in_specs=a_specs + [pl.BlockSpec((1, w3d.shape[1], tn), lambda i, j: (layer, 0, j)),
                            pl.BlockSpec((tm, tn), lambda i, j: (i, j))],
        out_specs=pl.BlockSpec((tm, tn), lambda i, j: (i, j)),
        out_shape=jax.ShapeDtypeStruct((m, n), F32),
        compiler_params=_params(("parallel", "arbitrary")),
        name="out_proj",
    )(*pieces, w3d, x2d)


def _mlp_up_kernel(x_ref, g_ref, w_ref, o_ref, h_ref):
    @pl.when(pl.program_id(1) == 0)
    def _():
        _norm_into(x_ref, g_ref, h_ref)

    r = jnp.maximum(_dot(h_ref[...], w_ref[0].astype(BF16)), 0.0)
    o_ref[...] = (r * r).astype(o_ref.dtype)


def _mlp_up(x2d, g, w3d, layer):
    m, k = x2d.shape
    n = w3d.shape[2]
    tm, tn = 1024, 512
    return pl.pallas_call(
        _mlp_up_kernel,
        grid=(m // tm, n // tn),
        in_specs=[pl.BlockSpec((tm, k), lambda i, j: (i, 0), pipeline_mode=pl.Buffered(1)),
                  pl.BlockSpec((1, k), lambda i, j: (0, 0)),
                  pl.BlockSpec((1, k, tn), lambda i, j: (layer, 0, j))],
        out_specs=pl.BlockSpec((tm, tn), lambda i, j: (i, j)),
        out_shape=jax.ShapeDtypeStruct((m, n), BF16),
        scratch_shapes=[pltpu.VMEM((tm, k), BF16)],
        compiler_params=_params(("parallel", "arbitrary")),
        name="mlp_up",
    )(x2d, g.reshape(1, k), w3d)


def _mlp_down_kernel(a_ref, w_ref, x_ref, o_ref):
    @pl.when(pl.program_id(2) == 0)
    def _():
        o_ref[...] = x_ref[...]

    o_ref[...] += _dot(a_ref[...], w_ref[0])


def _mlp_down(a, w3d, layer, x2d):
    m, kdim = a.shape
    n = w3d.shape[2]
    tm, tn, tk = 1024, 1024, 2048
    return pl.pallas_call(
        _mlp_down_kernel,
        grid=(m // tm, n // tn, kdim // tk),
        in_specs=[pl.BlockSpec((tm, tk), lambda i, j, k: (i, k)),
                  pl.BlockSpec((1, tk, tn), lambda i, j, k: (layer, k, j)),
                  pl.BlockSpec((tm, tn), lambda i, j, k: (i, j))],
        out_specs=pl.BlockSpec((tm, tn), lambda i, j, k: (i, j)),
        out_shape=jax.ShapeDtypeStruct((m, n), F32),
        compiler_params=_params(("parallel", "parallel", "arbitrary")),
        name="mlp_down",
    )(a, w3d, x2d)


def _rope_tables(seq):
    inv = 1.0 / (ROPE_THETA ** (jnp.arange(0, HEAD_DIM, 2, dtype=F32) / HEAD_DIM))
    ang = jnp.arange(seq, dtype=F32)[:, None] * inv[None, :]
    cos, sin = jnp.cos(ang), jnp.sin(ang)
    return jnp.concatenate([cos, cos], axis=-1), jnp.concatenate([-sin, sin], axis=-1)


def _nsa_mixer(proj_n, proj_b, batch, seq, pe_k, pe_v, w_ck1, w_ck2, w_cv1, w_cv2):
    nb = seq // NSA_BLOCK
    groups = NSA_KV_HEADS
    x_kv = proj_b[:B_GATE].reshape(2 * groups * batch * nb, NSA_BLOCK * HEAD_DIM)
    pe = jnp.stack([pe_k.reshape(1, -1), pe_v.reshape(1, -1)])
    w1 = jnp.stack([w_ck1, w_cv1]).astype(BF16)
    w2 = jnp.stack([w_ck2, w_cv2]).astype(BF16)
    cmp_kv = _compress(x_kv, pe, w1, w2)
    cmp_kv = cmp_kv.reshape(2, groups * batch, nb, HEAD_DIM)
    cmp_kv = jnp.pad(cmp_kv, ((0, 0), (0, 0), (0, NB_PAD - nb), (0, 0))).astype(BF16)
    kc = cmp_kv[0]
    vct = cmp_kv[1].transpose(0, 2, 1)
    return _nsa_attention(proj_n, kc, vct, proj_b, batch, seq)


def _layer(x2d, batch, seq, cosf, sinf, layer, w_in, w_tail, g_mix, w_out, g_mlp, w_up, w_down,
           pe_k, pe_v, w_ck1, w_ck2, w_cv1, w_cv2):
    q_tiles = DIL_HEADS * LANES // PROJ_TN
    sq_tile = CB_SQ * LANES // PROJ_TN
    proj_d, h = _proj_norm(x2d, g_mix, w_tail, layer, A_COLS // PROJ_TN, 2 * q_tiles, 0,
                           ((0, q_tiles - 1, SCALE_LOG2E),
                            (sq_tile, sq_tile + SB_HEADS * LANES // PROJ_TN - 1, SCALE_LOG2E)),
                           cosf, sinf, seq)
    proj_n = _proj(h, w_in, layer, lambda j: jnp.where(j >= 3, j + 1, j), N_COLS // PROJ_TN, 3, 2,
                   ((0, 2, SCALE_LOG2E),), cosf, sinf, seq)
    proj_b = _proj_b(h, w_in, layer, cosf, sinf, seq)

    o_a = _nsa_mixer(proj_n, proj_b, batch, seq, pe_k, pe_v, w_ck1, w_ck2, w_cv1, w_cv2)
    o_b = _dil_attention(proj_d, batch, seq)
    o_c = _sb_attention(proj_d, batch, seq)

    x2d = _out_proj([o_a, *o_b, o_c], w_out, layer, x2d)
    up = _mlp_up(x2d, g_mlp, w_up, layer)
    return _mlp_down(up, w_down, layer, x2d)


def kernel(x, w_in, w_out, norm_mix, norm_mlp, w_up, w_down, nsa_pe_k, nsa_pe_v,
           nsa_w_ck1, nsa_w_ck2, nsa_w_cv1, nsa_w_cv2, final_norm):
    batch, seq, d = x.shape
    cosf, sinf = _rope_tables(seq)
    depth = w_in.shape[0]
    w_head = jnp.transpose(w_in[:, :, :W_GATE + LANES], (0, 2, 1)).astype(BF16)
    w_tail = jnp.transpose(w_in[:, :, W_TAIL:], (0, 2, 1)).astype(BF16)
    w_down = w_down.astype(BF16)
    x2d = x.reshape(batch * seq, d)
    for layer in range(depth):
        x2d = _layer(x2d, batch, seq, cosf, sinf, layer, w_head, w_tail, norm_mix[layer], w_out,
                     norm_mlp[layer], w_up, w_down, nsa_pe_k[layer], nsa_pe_v[layer],
                     nsa_w_ck1[layer], nsa_w_ck2[layer], nsa_w_cv1[layer], nsa_w_cv2[layer])
    return _rmsnorm(x2d, final_norm, x.dtype).reshape(batch, seq, d)
```
